```python
import jax, jax.numpy as jnp
from jax import lax
import numpy as np


D_MODEL = 1024
BATCH = 32
SEQ = 2048
DEPTH = 4

MEM_LEN = 256
HEAD_DIM = D_MODEL // 16
N_RET_HEADS = 8
N_ATT_HEADS = 8
RET_WIDTH = N_RET_HEADS * HEAD_DIM
ATT_WIDTH = N_ATT_HEADS * HEAD_DIM
MIX_WIDTH = RET_WIDTH + ATT_WIDTH
IN_WIDTH = 4 * RET_WIDTH + 3 * ATT_WIDTH
RET_CHUNK = 128
DILATED_PATTERNS = ((128, 1), (512, 4), (2048, 16))
ATT_BLOCK = 64
N_MEM_HEADS = 4
MEM_HEAD_DIM = D_MODEL // N_MEM_HEADS
PEER_HEADS = 8
PEER_N_KEYS = 128
PEER_N_EXPERTS = PEER_N_KEYS * PEER_N_KEYS
PEER_TOPK = 16
PEER_QUERY_DIM = 256
PEER_HALF = PEER_QUERY_DIM // 2
PEER_TOKEN_BLOCK = 128
RMS_EPS = 1e-6
NEG_INF = -1e30

kernel_name = 'hybrid_retention_dilated_peer_encoder'


def rms_norm(x, gain):
    xf = x.astype(jnp.float32)
    y = xf * lax.rsqrt(jnp.mean(xf * xf, axis=-1, keepdims=True) + RMS_EPS)
    return (y * gain.astype(jnp.float32)).astype(x.dtype)


def split_heads(t, n_heads):
    b, s, _ = t.shape
    return t.reshape(b, s, n_heads, -1).transpose(0, 2, 1, 3)


def head_rms_norm(o, gain):
    b, h, s, d = o.shape
    of = o.astype(jnp.float32)
    y = of * lax.rsqrt(jnp.mean(of * of, axis=-1, keepdims=True) + RMS_EPS)
    y = y.transpose(0, 2, 1, 3).reshape(b, s, h * d)
    return (y * gain.astype(jnp.float32)).astype(o.dtype)


def alibi_slopes(n_heads):
    return 2.0 ** (-8.0 * (jnp.arange(n_heads, dtype=jnp.float32) + 1.0) / n_heads)


def retention_scan(q, k, v, log_gamma, include_diag):
    b, h, s, d = q.shape
    dt = q.dtype
    n_chunks = s // RET_CHUNK
    pos = jnp.arange(RET_CHUNK, dtype=jnp.float32)
    rel = pos[:, None] - pos[None, :]
    mask = (rel >= 0) if include_diag else (rel > 0)
    decay = jnp.where(mask[None], jnp.exp(log_gamma[:, None, None] * jnp.where(mask, rel, 0.0)[None]), 0.0).astype(dt)
    xi = jnp.exp(log_gamma[:, None] * (pos[None] + 1.0)).astype(dt)
    zeta = jnp.exp(log_gamma[:, None] * (RET_CHUNK - 1.0 - pos[None])).astype(dt)
    chunk_decay = jnp.exp(log_gamma * RET_CHUNK).astype(dt)

    def to_chunks(t):
        return jnp.moveaxis(t.reshape(b, h, n_chunks, RET_CHUNK, t.shape[-1]), 2, 0)

    def step(state, qkv):
        qc, kc, vc = qkv
        inter = jnp.einsum('bhcd,bhde->bhce', qc, state) * xi[None, :, :, None]
        scores = jnp.einsum('bhcd,bhsd->bhcs', qc, kc) * decay[None]
        out = inter + jnp.einsum('bhcs,bhse->bhce', scores, vc)
        state = state * chunk_decay[None, :, None, None] + jnp.einsum('bhsd,bhse->bhde', kc * zeta[None, :, :, None], vc)
        return state, out

    state0 = jnp.zeros((b, h, d, v.shape[-1]), dt)
    _, out = lax.scan(step, state0, (to_chunks(q), to_chunks(k), to_chunks(v)))
    return jnp.moveaxis(out, 0, 2).reshape(b, h, s, v.shape[-1])


def bidirectional_retention(q, k, v, log_gamma):
    fwd = retention_scan(q, k, v, log_gamma[0], True)
    bwd = retention_scan(jnp.flip(q, 2), jnp.flip(k, 2), jnp.flip(v, 2), log_gamma[1], False)
    return fwd + jnp.flip(bwd, 2)


def dilated_band_attention(q, k, v, slopes, dilation, half_span):
    b, h, s, d = q.shape
    dt = q.dtype
    sub_len = s // dilation
    n_blocks = -(-sub_len // ATT_BLOCK)
    padded_len = n_blocks * ATT_BLOCK

    def to_sub(t):
        return t.reshape(b, h, sub_len, dilation, d).transpose(0, 1, 3, 2, 4)

    qs = jnp.pad(to_sub(q), ((0, 0), (0, 0), (0, 0), (0, padded_len - sub_len), (0, 0)))
    kv_pad = ((0, 0), (0, 0), (0, 0), (half_span, padded_len - sub_len + half_span), (0, 0))
    ks = jnp.pad(to_sub(k), kv_pad)
    vs = jnp.pad(to_sub(v), kv_pad)
    key_width = ATT_BLOCK + 2 * half_span
    key_idx = jnp.arange(n_blocks)[:, None] * ATT_BLOCK + jnp.arange(key_width)[None, :]
    kb = ks[:, :, :, key_idx]
    vb = vs[:, :, :, key_idx]
    qb = qs.reshape(b, h, dilation, n_blocks, ATT_BLOCK, d)
    scores = jnp.einsum('bhrnqd,bhrnkd->bhrnqk', qb, kb).astype(jnp.float32) * (d ** -0.5)
    q_pos = jnp.arange(n_blocks)[:, None] * ATT_BLOCK + jnp.arange(ATT_BLOCK)[None, :]
    k_pos = key_idx - half_span
    rel = k_pos[:, None, :] - q_pos[:, :, None]
    valid = (jnp.abs(rel) <= half_span) & (k_pos[:, None, :] >= 0) & (k_pos[:, None, :] < sub_len)
    dist = (jnp.abs(rel) * dilation).astype(jnp.float32)
    bias = -slopes[:, None, None, None] * dist[None]
    scores = jnp.where(valid[None, None, None], scores + bias[None, :, None], NEG_INF)
    m = jnp.max(scores, axis=-1, keepdims=True)
    p = jnp.exp(scores - m)
    l = jnp.sum(p, axis=-1, keepdims=True)
    out = jnp.einsum('bhrnqk,bhrnkd->bhrnqd', (p / l).astype(dt), vb)
    lse = (m + jnp.log(l))[..., 0]
    out = out.reshape(b, h, dilation, padded_len, d)[:, :, :, :sub_len].transpose(0, 1, 3, 2, 4).reshape(b, h, s, d)
    lse = lse.reshape(b, h, dilation, padded_len)[..., :sub_len].transpose(0, 1, 3, 2).reshape(b, h, s)
    return out, lse


def dilated_mixture_attention(q, k, v):
    slopes = alibi_slopes(q.shape[1])
    outs, lses = [], []
    for window, dilation in DILATED_PATTERNS:
        o, l = dilated_band_attention(q, k, v, slopes, dilation, window // (2 * dilation))
        outs.append(o)
        lses.append(l)
    weights = jax.nn.softmax(jnp.stack(lses, 0), axis=0).astype(q.dtype)
    return jnp.sum(weights[..., None] * jnp.stack(outs, 0), axis=0)


def memory_cross_attention(h, mem_n, w_q, w_kv, w_o):
    b, s, _ = h.shape
    q = (h @ w_q).reshape(b, s, N_MEM_HEADS, MEM_HEAD_DIM)
    k, v = jnp.split(mem_n @ w_kv, 2, axis=-1)
    k = k.reshape(b, -1, N_MEM_HEADS, MEM_HEAD_DIM)
    v = v.reshape(b, -1, N_MEM_HEADS, MEM_HEAD_DIM)
    scores = jnp.einsum('bshd,bmhd->bhsm', q, k).astype(jnp.float32) * (MEM_HEAD_DIM ** -0.5)
    p = jax.nn.softmax(scores, axis=-1).astype(h.dtype)
    o = jnp.einsum('bhsm,bmhd->bshd', p, v).reshape(b, s, D_MODEL)
    return o @ w_o


def peer_ffn(h, w_query, sub_keys, expert_down, expert_up):
    b, s, dm = h.shape
    dt = h.dtype
    tokens = h.reshape(-1, PEER_TOKEN_BLOCK, dm)
    keys_f = sub_keys.astype(jnp.float32)

    def block(xb):
        q = (xb @ w_query).reshape(PEER_TOKEN_BLOCK, PEER_HEADS, 2, PEER_HALF).astype(jnp.float32)
        half_scores = jnp.einsum('thcd,ckd->thck', q, keys_f)
        top_s, top_i = lax.top_k(half_scores, PEER_TOPK)
        cand_s = top_s[:, :, 0, :, None] + top_s[:, :, 1, None, :]
        cand_i = top_i[:, :, 0, :, None] * PEER_N_KEYS + top_i[:, :, 1, None, :]
        best_s, best_pos = lax.top_k(cand_s.reshape(PEER_TOKEN_BLOCK, PEER_HEADS, -1), PEER_TOPK)
        expert_idx = jnp.take_along_axis(cand_i.reshape(PEER_TOKEN_BLOCK, PEER_HEADS, -1), best_pos, axis=-1)
        gates = jax.nn.softmax(best_s, axis=-1).astype(dt)
        u = expert_down[expert_idx]
        v = expert_up[expert_idx]
        act = jax.nn.gelu(jnp.einsum('thkd,td->thk', u, xb))
        return jnp.einsum('thk,thkd->td', gates * act, v)

    return lax.map(block, tokens).reshape(b, s, dm)


def setup_inputs(seed: int = 0) -> dict:
    key = jax.random.key(seed)
    ks = jax.random.split(key, 20)
    f32 = jnp.float32

    def normal(k, shape, scale):
        return jax.random.normal(k, shape, f32) * scale

    def gain(k, shape):
        return 1.0 + 0.05 * jax.random.normal(k, shape, f32)

    base_logit = jnp.log(2.0 ** (5.0 + jnp.arange(N_RET_HEADS, dtype=f32)) - 1.0)
    return {
        'x': normal(ks[0], (BATCH, SEQ, D_MODEL), 1.0),
        'mem': normal(ks[1], (BATCH, MEM_LEN, D_MODEL), 1.0),
        'norm_mix': gain(ks[2], (DEPTH, D_MODEL)),
        'w_in': normal(ks[3], (DEPTH, D_MODEL, IN_WIDTH), D_MODEL ** -0.5),
        'ret_decay_logit': base_logit[None, None, :] + 0.1 * jax.random.normal(ks[4], (DEPTH, 2, N_RET_HEADS), f32),
        'ret_norm_gain': gain(ks[5], (DEPTH, RET_WIDTH)),
        'att_norm_gain': gain(ks[6], (DEPTH, ATT_WIDTH)),
        'w_out': normal(ks[7], (DEPTH, MIX_WIDTH, D_MODEL), MIX_WIDTH ** -0.5),
        'norm_mem': gain(ks[8], (DEPTH, D_MODEL)),
        'norm_mem_kv': gain(ks[9], (DEPTH, D_MODEL)),
        'w_mem_q': normal(ks[10], (DEPTH, D_MODEL, D_MODEL), D_MODEL ** -0.5),
        'w_mem_kv': normal(ks[11], (DEPTH, D_MODEL, 2 * D_MODEL), D_MODEL ** -0.5),
        'w_mem_o': normal(ks[12], (DEPTH, D_MODEL, D_MODEL), D_MODEL ** -0.5),
        'norm_ffn': gain(ks[13], (DEPTH, D_MODEL)),
        'peer_w_query': normal(ks[14], (DEPTH, D_MODEL, PEER_HEADS * PEER_QUERY_DIM), D_MODEL ** -0.5),
        'peer_sub_keys': normal(ks[15], (DEPTH, 2, PEER_N_KEYS, PEER_HALF), PEER_HALF ** -0.5),
        'peer_expert_down': normal(ks[16], (DEPTH, PEER_N_EXPERTS, D_MODEL), D_MODEL ** -0.5),
        'peer_expert_up': normal(ks[17], (DEPTH, PEER_N_EXPERTS, D_MODEL), (PEER_HEADS * PEER_TOPK) ** -0.5),
        'norm_final': gain(ks[18], (D_MODEL,)),
    }


def reference(x, mem, norm_mix, w_in, ret_decay_logit, ret_norm_gain, att_norm_gain, w_out,
              norm_mem, norm_mem_kv, w_mem_q, w_mem_kv, w_mem_o, norm_ffn,
              peer_w_query, peer_sub_keys, peer_expert_down, peer_expert_up, norm_final):
    split_points = [RET_WIDTH, 2 * RET_WIDTH, 3 * RET_WIDTH, 4 * RET_WIDTH,
                    4 * RET_WIDTH + ATT_WIDTH, 4 * RET_WIDTH + 2 * ATT_WIDTH]
    for layer in range(DEPTH):
        h = rms_norm(x, norm_mix[layer])
        proj = h @ w_in[layer]
        rq, rk, rv, rg, aq, ak, av = jnp.split(proj, split_points, axis=-1)
        log_gamma = jax.nn.log_sigmoid(ret_decay_logit[layer].astype(jnp.float32))
        ret = bidirectional_retention(split_heads(rq, N_RET_HEADS),
                                      split_heads(rk, N_RET_HEADS) * (HEAD_DIM ** -0.5),
                                      split_heads(rv, N_RET_HEADS), log_gamma)
        ret_out = head_rms_norm(ret, ret_norm_gain[layer]) * jax.nn.silu(rg)
        att = dilated_mixture_attention(split_heads(aq, N_ATT_HEADS), split_heads(ak, N_ATT_HEADS),
                                        split_heads(av, N_ATT_HEADS))
        att_out = head_rms_norm(att, att_norm_gain[layer])
        x = x + jnp.concatenate([ret_out, att_out], axis=-1) @ w_out[layer]
        h = rms_norm(x, norm_mem[layer])
        mem_n = rms_norm(mem, norm_mem_kv[layer])
        x = x + memory_cross_attention(h, mem_n, w_mem_q[layer], w_mem_kv[layer], w_mem_o[layer])
        h = rms_norm(x, norm_ffn[layer])
        x = x + peer_ffn(h, peer_w_query[layer], peer_sub_keys[layer],
                         peer_expert_down[layer], peer_expert_up[layer])
    return rms_norm(x, norm_final)
```

```python
import functools
import math

import jax
import jax.numpy as jnp
from jax import lax
from jax.experimental import pallas as pl
from jax.experimental.pallas import tpu as pltpu

F32 = jnp.float32
BF16 = jnp.bfloat16

D_MODEL = 1024
HEAD_DIM = 64
N_RET_HEADS = 8
N_ATT_HEADS = 8
RET_WIDTH = N_RET_HEADS * HEAD_DIM
ATT_WIDTH = N_ATT_HEADS * HEAD_DIM
RET_PROJ = 4 * RET_WIDTH
ATT_PROJ = 3 * ATT_WIDTH
RET_CHUNK = 128
DILATIONS = (1, 4, 16)
HALF_SPAN = 64
N_MEM_HEADS = 4
MEM_HEAD_DIM = D_MODEL // N_MEM_HEADS
PEER_HEADS = 8
PEER_N_KEYS = 128
PEER_TOPK = 16
PEER_QUERY_DIM = 256
PEER_HALF = PEER_QUERY_DIM // 2
RMS_EPS = 1e-6
NEG_INF = -1e30
LANES = 128
ATT_Q_BLOCK = 128
ATT_K_BLOCK = ATT_Q_BLOCK + 2 * HALF_SPAN
VMEM_LIMIT = 56 * 1024 * 1024

NT_DIMS = (((1,), (1,)), ((), ()))


def _params(sem):
    return pltpu.CompilerParams(dimension_semantics=sem, vmem_limit_bytes=VMEM_LIMIT)


def _rms(x, gain):
    ms = jnp.mean(x * x, axis=-1, keepdims=True)
    return x * lax.rsqrt(ms + RMS_EPS) * gain


def _pair_head_norm(y, gain, lane_lo):
    y2 = y * y
    ms0 = jnp.sum(jnp.where(lane_lo, y2, 0.0), axis=-1, keepdims=True)
    ms1 = jnp.sum(jnp.where(lane_lo, 0.0, y2), axis=-1, keepdims=True)
    ms = jnp.where(lane_lo, ms0, ms1) * (1.0 / HEAD_DIM)
    return y * lax.rsqrt(ms + RMS_EPS) * gain


def _in_proj_kernel(x_ref, g_ref, wr_ref, wa_ref, or_ref, oa_ref):
    h = _rms(x_ref[...], g_ref[...]).astype(BF16)
    or_ref[...] = jnp.dot(h, wr_ref[...], preferred_element_type=F32).astype(or_ref.dtype)
    oa_ref[...] = jnp.dot(h, wa_ref[...], preferred_element_type=F32).astype(oa_ref.dtype)


def _in_proj(x2, gain, w_ret, w_att, tm=512):
    n = x2.shape[0]
    return pl.pallas_call(
        _in_proj_kernel,
        grid=(n // tm,),
        in_specs=[
            pl.BlockSpec((tm, D_MODEL), lambda i: (i, 0)),
            pl.BlockSpec((1, D_MODEL), lambda i: (0, 0)),
            pl.BlockSpec((D_MODEL, RET_PROJ), lambda i: (0, 0)),
            pl.BlockSpec((D_MODEL, ATT_PROJ), lambda i: (0, 0)),
        ],
        out_specs=[
            pl.BlockSpec((tm, RET_PROJ), lambda i: (i, 0)),
            pl.BlockSpec((tm, ATT_PROJ), lambda i: (i, 0)),
        ],
        out_shape=[
            jax.ShapeDtypeStruct((n, RET_PROJ), BF16),
            jax.ShapeDtypeStruct((n, ATT_PROJ), F32),
        ],
        compiler_params=_params(("parallel",)),
        name="in_proj",
    )(x2, gain.reshape(1, D_MODEL), w_ret, w_att)


def _ret_kernel(lg_ref, gain_ref, q_ref, k_ref, v_ref, g_ref, o_ref, acc_ref, *, seq):
    c_len = RET_CHUNK
    n_chunks = seq // c_len
    lg = jax.nn.log_sigmoid(lg_ref[0])
    lgf = lg[0:1, :]
    lgb = lg[1:2, :]
    row = lax.broadcasted_iota(jnp.int32, (c_len, LANES), 0)
    lane = lax.broadcasted_iota(jnp.int32, (c_len, LANES), 1)
    lane_lo = lane < HEAD_DIM
    block_diag = (row < HEAD_DIM) == lane_lo
    rowf = row.astype(F32)
    rel = (row - lane).astype(F32)

    def decay_matrix(col):
        f = jnp.exp(lgf[:, col:col + 1] * jnp.maximum(rel, 0.0))
        b = jnp.exp(lgb[:, col:col + 1] * jnp.maximum(-rel, 0.0))
        return jnp.where(rel >= 0, f, b)

    decay2 = jnp.concatenate([decay_matrix(0), decay_matrix(HEAD_DIM)], axis=0)
    xi_f = jnp.exp(lgf * (rowf + 1.0))
    zeta_f = jnp.exp(lgf * (c_len - 1.0 - rowf))
    cd_f = jnp.exp(lgf * float(c_len))
    xi_b = jnp.exp(lgb * (c_len - rowf))
    zeta_b = jnp.exp(lgb * rowf)
    cd_b = jnp.exp(lgb * float(c_len))

    def state_update(state, ks, vc, zeta, cd):
        kz_t = (ks * zeta).T.astype(BF16)
        upd = jnp.dot(kz_t, vc, preferred_element_type=F32)
        return state * cd + jnp.where(block_diag, upd, 0.0)

    def fwd_step(c, state):
        sl = pl.ds(pl.multiple_of(c * c_len, c_len), c_len)
        qc = q_ref[sl, :]
        vc = v_ref[sl, :]
        ks = k_ref[sl, :].astype(F32) * (HEAD_DIM ** -0.5)
        q2 = jnp.concatenate([jnp.where(lane_lo, qc, jnp.zeros_like(qc)),
                              jnp.where(lane_lo, jnp.zeros_like(qc), qc)], axis=0)
        s = lax.dot_general(q2, ks.astype(BF16), NT_DIMS, preferred_element_type=F32)
        p = (s * decay2).astype(BF16)
        o2 = jnp.dot(p, vc, preferred_element_type=F32)
        intra = jnp.where(lane_lo, o2[:c_len], o2[c_len:])
        inter = jnp.dot(qc, state.astype(BF16), preferred_element_type=F32) * xi_f
        acc_ref[sl, :] = intra + inter
        return state_update(state, ks, vc, zeta_f, cd_f)

    zero_state = jnp.zeros((LANES, LANES), F32)
    lax.fori_loop(0, n_chunks, fwd_step, zero_state)

    gain = gain_ref[...]

    def bwd_step(i, state):
        c = n_chunks - 1 - i
        sl = pl.ds(pl.multiple_of(c * c_len, c_len), c_len)
        qc = q_ref[sl, :]
        vc = v_ref[sl, :]
        ks = k_ref[sl, :].astype(F32) * (HEAD_DIM ** -0.5)
        y = acc_ref[sl, :] + jnp.dot(qc, state.astype(BF16), preferred_element_type=F32) * xi_b
        yn = _pair_head_norm(y, gain, lane_lo)
        o_ref[sl, :] = (yn * jax.nn.silu(g_ref[sl, :].astype(F32))).astype(o_ref.dtype)
        return state_update(state, ks, vc, zeta_b, cd_b)

    lax.fori_loop(0, n_chunks, bwd_step, zero_state)


def _retention(proj_ret, decay_logit, gain, batch, seq):
    n_pairs = N_RET_HEADS // 2
    lg = jnp.repeat(decay_logit.astype(F32), HEAD_DIM, axis=-1)
    lg = lg.reshape(2, n_pairs, LANES).transpose(1, 0, 2)
    col = lambda off: (lambda b, p: (b, 0, off + p))
    blk = lambda off: pl.BlockSpec((None, seq, LANES), col(off))
    return pl.pallas_call(
        functools.partial(_ret_kernel, seq=seq),
        grid=(batch, n_pairs),
        in_specs=[
            pl.BlockSpec((1, 2, LANES), lambda b, p: (p, 0, 0)),
            pl.BlockSpec((1, LANES), lambda b, p: (0, p)),
            blk(0), blk(n_pairs), blk(2 * n_pairs), blk(3 * n_pairs),
        ],
        out_specs=pl.BlockSpec((None, seq, LANES), lambda b, p: (b, 0, p)),
        out_shape=jax.ShapeDtypeStruct((batch, seq, RET_WIDTH), BF16),
        scratch_shapes=[pltpu.VMEM((seq, LANES), F32)],
        compiler_params=_params(("parallel", "parallel")),
        name="retention",
    )(lg, gain.reshape(1, RET_WIDTH), proj_ret, proj_ret, proj_ret, proj_ret)


def _att_kernel(gain_ref, q_ref, k_ref, v_ref, o_ref,
                qd_ref, kd_ref, vd_ref, od_ref, ld_ref, op_ref, lp_ref, *, seq):
    qb_len = ATT_Q_BLOCK
    kb_len = ATT_K_BLOCK
    n_blocks = seq // qb_len
    pair = pl.program_id(1)
    lane = lax.broadcasted_iota(jnp.int32, (qb_len, LANES), 1)
    lane_lo = lane < HEAD_DIM

    srow = lax.broadcasted_iota(jnp.int32, (2 * qb_len, kb_len), 0)
    scol = lax.broadcasted_iota(jnp.int32, (2 * qb_len, kb_len), 1)
    head_in_pair = (srow >= qb_len).astype(F32)
    qrow = jnp.where(srow >= qb_len, srow - qb_len, srow)
    dist = jnp.abs(scol - HALF_SPAN - qrow)
    in_band = dist <= HALF_SPAN
    slope = jnp.exp(-math.log(2.0) * (2.0 * pair.astype(F32) + 1.0 + head_in_pair))
    base_bias = -slope * dist.astype(F32)
    kcol = lax.broadcasted_iota(jnp.int32, (1, kb_len), 1)

    zeros_pad = jnp.zeros((HALF_SPAN, LANES), F32)

    for pat, dil in enumerate(DILATIONS):
        sub_len = seq // dil
        cls_len = sub_len + 2 * HALF_SPAN
        blocks_per_cls = sub_len // qb_len
        bias = jnp.where(in_band, base_bias * float(dil), NEG_INF)

        for r in range(dil):
            kd_ref[r * cls_len:r * cls_len + HALF_SPAN, :] = zeros_pad
            vd_ref[r * cls_len:r * cls_len + HALF_SPAN, :] = zeros_pad
            kd_ref[r * cls_len + HALF_SPAN + sub_len:(r + 1) * cls_len, :] = zeros_pad
            vd_ref[r * cls_len + HALF_SPAN + sub_len:(r + 1) * cls_len, :] = zeros_pad
            for j in range(blocks_per_cls):
                if dil == 1:
                    src = pl.ds(j * qb_len, qb_len)
                else:
                    src = pl.ds(r + j * qb_len * dil, qb_len, stride=dil)
                dst_q = pl.ds(r * sub_len + j * qb_len, qb_len)
                dst_k = pl.ds(r * cls_len + HALF_SPAN + j * qb_len, qb_len)
                qd_ref[dst_q, :] = q_ref[src, :]
                kd_ref[dst_k, :] = k_ref[src, :]
                vd_ref[dst_k, :] = v_ref[src, :]

        def block_step(i, carry, cls_len=cls_len, sub_len=sub_len,
                       blocks_per_cls=blocks_per_cls, bias=bias):
            r = i // blocks_per_cls
            m = i - r * blocks_per_cls
            q_off = pl.multiple_of(i * qb_len, qb_len)
            k_off = pl.multiple_of(r * cls_len + m * qb_len, HALF_SPAN)
            qb = qd_ref[pl.ds(q_off, qb_len), :] * (HEAD_DIM ** -0.5)
            q2 = jnp.concatenate([jnp.where(lane_lo, qb, 0.0),
                                  jnp.where(lane_lo, 0.0, qb)], axis=0).astype(BF16)
            kw = kd_ref[pl.ds(k_off, kb_len), :].astype(BF16)
            vw = vd_ref[pl.ds(k_off, kb_len), :].astype(BF16)
            s = lax.dot_general(q2, kw, NT_DIMS, preferred_element_type=F32)
            kpos = kcol + (m * qb_len - HALF_SPAN)
            valid = (kpos >= 0) & (kpos < sub_len)
            s = jnp.where(valid, s + bias, NEG_INF)
            mx = jnp.max(s, axis=-1, keepdims=True)
            p = jnp.exp(s - mx)
            l = jnp.sum(p, axis=-1, keepdims=True)
            o2 = jnp.dot(p.astype(BF16), vw, preferred_element_type=F32) / l
            lse = mx + jnp.log(l)
            od_ref[pl.ds(q_off, qb_len), :] = jnp.where(lane_lo, o2[:qb_len], o2[qb_len:])
            ld_ref[pl.ds(q_off, qb_len), :] = jnp.where(
                lane_lo, jnp.broadcast_to(lse[:qb_len], (qb_len, LANES)),
                jnp.broadcast_to(lse[qb_len:], (qb_len, LANES)))
            return carry

        lax.fori_loop(0, n_blocks, block_step, 0)

        for r in range(dil):
            for j in range(blocks_per_cls):
                src = pl.ds(r * sub_len + j * qb_len, qb_len)
                if dil == 1:
                    dst = pl.ds(j * qb_len, qb_len)
                else:
                    dst = pl.ds(r + j * qb_len * dil, qb_len, stride=dil)
                op_ref[pat, dst, :] = od_ref[src, :]
                lp_ref[pat, dst, :] = ld_ref[src, :]

    gain = gain_ref[...]

    def mix_step(i, carry):
        sl = pl.ds(pl.multiple_of(i * qb_len, qb_len), qb_len)
        l0 = lp_ref[0, sl, :]
        l1 = lp_ref[1, sl, :]
        l2 = lp_ref[2, sl, :]
        mm = jnp.maximum(jnp.maximum(l0, l1), l2)
        w0 = jnp.exp(l0 - mm)
        w1 = jnp.exp(l1 - mm)
        w2 = jnp.exp(l2 - mm)
        y = (w0 * op_ref[0, sl, :] + w1 * op_ref[1, sl, :] + w2 * op_ref[2, sl, :]) / (w0 + w1 + w2)
        o_ref[sl, :] = _pair_head_norm(y, gain, lane_lo).astype(o_ref.dtype)
        return carry

    lax.fori_loop(0, n_blocks, mix_step, 0)


def _dilated_attention(proj_att, gain, batch, seq):
    n_pairs = N_ATT_HEADS // 2
    max_cls_rows = max(d * (seq // d + 2 * HALF_SPAN) for d in DILATIONS)
    blk = lambda off: pl.BlockSpec((None, seq, LANES), lambda b, p: (b, 0, off + p))
    return pl.pallas_call(
        functools.partial(_att_kernel, seq=seq),
        grid=(batch, n_pairs),
        in_specs=[
            pl.BlockSpec((1, LANES), lambda b, p: (0, p)),
            blk(0), blk(n_pairs), blk(2 * n_pairs),
        ],
        out_specs=pl.BlockSpec((None, seq, LANES), lambda b, p: (b, 0, p)),
        out_shape=jax.ShapeDtypeStruct((batch, seq, ATT_WIDTH), BF16),
        scratch_shapes=[
            pltpu.VMEM((seq, LANES), F32),
            pltpu.VMEM((max_cls_rows, LANES), F32),
            pltpu.VMEM((max_cls_rows, LANES), F32),
            pltpu.VMEM((seq, LANES), F32),
            pltpu.VMEM((seq, LANES), F32),
            pltpu.VMEM((len(DILATIONS), seq, LANES), F32),
            pltpu.VMEM((len(DILATIONS), seq, LANES), F32),
        ],
        compiler_params=_params(("parallel", "parallel")),
        name="dilated_attention",
    )(gain.reshape(1, ATT_WIDTH), proj_att, proj_att, proj_att)


def _out_proj_kernel(x_ref, a_ref, b_ref, wa_ref, wb_ref, o_ref):
    y = jnp.dot(a_ref[...], wa_ref[...], preferred_element_type=F32)
    y = y + jnp.dot(b_ref[...], wb_ref[...], preferred_element_type=F32)
    o_ref[...] = x_ref[...] + y


def _out_proj(x2, ret_out, att_out, w_a, w_b, tm=512):
    n = x2.shape[0]
    return pl.pallas_call(
        _out_proj_kernel,
        grid=(n // tm,),
        in_specs=[
            pl.BlockSpec((tm, D_MODEL), lambda i: (i, 0)),
            pl.BlockSpec((tm, RET_WIDTH), lambda i: (i, 0)),
            pl.BlockSpec((tm, ATT_WIDTH), lambda i: (i, 0)),
            pl.BlockSpec((RET_WIDTH, D_MODEL), lambda i: (0, 0)),
            pl.BlockSpec((ATT_WIDTH, D_MODEL), lambda i: (0, 0)),
        ],
        out_specs=pl.BlockSpec((tm, D_MODEL), lambda i: (i, 0)),
        out_shape=jax.ShapeDtypeStruct((n, D_MODEL), F32),
        compiler_params=_params(("parallel",)),
        name="out_proj",
    )(x2, ret_out, att_out, w_a, w_b)


def _mem_kv_kernel(m_ref, g_ref, w_ref, o_ref):
    h = _rms(m_ref[...], g_ref[...]).astype(BF16)
    o_ref[...] = jnp.dot(h, w_ref[...], preferred_element_type=F32).astype(o_ref.dtype)


def _mem_kv(mem2, gain, w_kv, tm=512):
    n = mem2.shape[0]
    return pl.pallas_call(
        _mem_kv_kernel,
        grid=(n // tm,),
        in_specs=[
            pl.BlockSpec((tm, D_MODEL), lambda i: (i, 0)),
            pl.BlockSpec((1, D_MODEL), lambda i: (0, 0)),
            pl.BlockSpec((D_MODEL, 2 * D_MODEL), lambda i: (0, 0)),
        ],
        out_specs=pl.BlockSpec((tm, 2 * D_MODEL), lambda i: (i, 0)),
        out_shape=jax.ShapeDtypeStruct((n, 2 * D_MODEL), BF16),
        compiler_params=_params(("parallel",)),
        name="mem_kv",
    )(mem2, gain.reshape(1, D_MODEL), w_kv)


def _mem_attn_kernel(x_ref, g_ref, wq_ref, kv_ref, wo_ref, o_ref):
    x = x_ref[...]
    h = _rms(x, g_ref[...]).astype(BF16)
    q = jnp.dot(h, wq_ref[...], preferred_element_type=F32) * (MEM_HEAD_DIM ** -0.5)
    q = q.astype(BF16)
    outs = []
    for hd in range(N_MEM_HEADS):
        lo = hd * MEM_HEAD_DIM
        kh = kv_ref[:, lo:lo + MEM_HEAD_DIM]
        vh = kv_ref[:, D_MODEL + lo:D_MODEL + lo + MEM_HEAD_DIM]
        s = lax.dot_general(q[:, lo:lo + MEM_HEAD_DIM], kh, NT_DIMS, preferred_element_type=F32)
        mx = jnp.max(s, axis=-1, keepdims=True)
        p = jnp.exp(s - mx)
        p = p / jnp.sum(p, axis=-1, keepdims=True)
        outs.append(jnp.dot(p.astype(BF16), vh, preferred_element_type=F32).astype(BF16))
    o = jnp.concatenate(outs, axis=-1)
    o_ref[...] = x + jnp.dot(o, wo_ref[...], preferred_element_type=F32)


def _mem_attn(x3, gain, w_q, kv3, w_o, tm=512):
    batch, seq, _ = x3.shape
    mem_len = kv3.shape[1]
    return pl.pallas_call(
        _mem_attn_kernel,
        grid=(batch, seq // tm),
        in_specs=[
            pl.BlockSpec((None, tm, D_MODEL), lambda b, i: (b, i, 0)),
            pl.BlockSpec((1, D_MODEL), lambda b, i: (0, 0)),
            pl.BlockSpec((D_MODEL, D_MODEL), lambda b, i: (0, 0)),
            pl.BlockSpec((None, mem_len, 2 * D_MODEL), lambda b, i: (b, 0, 0)),
            pl.BlockSpec((D_MODEL, D_MODEL), lambda b, i: (0, 0)),
        ],
        out_specs=pl.BlockSpec((None, tm, D_MODEL), lambda b, i: (b, i, 0)),
        out_shape=jax.ShapeDtypeStruct((batch, seq, D_MODEL), F32),
        compiler_params=_params(("parallel", "parallel")),
        name="mem_attn",
    )(x3, gain.reshape(1, D_MODEL), w_q, kv3, w_o)


def _candidate_pairs():
    return [(p, q) for p in range(PEER_TOPK) for q in range(PEER_TOPK)
            if (p + 1) * (q + 1) <= PEER_TOPK]


def _peer_kernel(x_ref, g_ref, wqt_ref, k0_ref, k1_ref, u_ref, vt_ref, o_ref,
                 hb_ref, s1_ref, r2_ref, b_ref, c_ref, a_ref, t1_ref, t2_ref, acc_ref,
                 *, tm, ec):
    e = pl.program_id(1)
    n_chunks = pl.num_programs(1)
    slabs = ec // PEER_N_KEYS
    lowest = float(jnp.finfo(jnp.float32).min)

    @pl.when(e == 0)
    def _select():
        hb_ref[...] = _rms(x_ref[...], g_ref[...]).astype(BF16)
        hb = hb_ref[...]
        for hd in range(PEER_HEADS):
            lo = hd * PEER_QUERY_DIM
            qt = lax.dot_general(wqt_ref[lo:lo + PEER_QUERY_DIM, :], hb, NT_DIMS,
                                 preferred_element_type=F32).astype(BF16)
            s1 = jnp.dot(k0_ref[...], qt[:PEER_HALF], preferred_element_type=F32)
            s2 = jnp.dot(k1_ref[...], qt[PEER_HALF:], preferred_element_type=F32)
            s1_ref[hd] = s1
            w = s1
            for it in range(PEER_TOPK):
                mx = jnp.max(w, axis=0, keepdims=True)
                t1_ref[it, hd:hd + 1, :] = mx
                w = jnp.where(w == mx, lowest, w)
            w = s2
            rank = jnp.full(s2.shape, float(PEER_N_KEYS), F32)
            for it in range(PEER_TOPK):
                mx = jnp.max(w, axis=0, keepdims=True)
                t2_ref[it, hd:hd + 1, :] = mx
                hit = w == mx
                rank = jnp.where(hit, float(it), rank)
                w = jnp.where(hit, lowest, w)
            r2_ref[hd] = rank.astype(BF16)
            b_ref[hd] = jnp.exp(s2 - t2_ref[0, hd:hd + 1, :]).astype(BF16)

        top = t1_ref[0] + t2_ref[0]
        cands = [t1_ref[p] + t2_ref[q] for p, q in _candidate_pairs()]
        z = jnp.zeros_like(top)
        tau = top
        for it in range(PEER_TOPK):
            mx = cands[0]
            for cnd in cands[1:]:
                mx = jnp.maximum(mx, cnd)
            z = z + jnp.exp(mx - top)
            tau = mx
            if it + 1 < PEER_TOPK:
                cands = [jnp.where(cnd == mx, lowest, cnd) for cnd in cands]
        inv_z = 1.0 / z

        for hd in range(PEER_HEADS):
            s1 = s1_ref[hd]
            tau_h = tau[hd:hd + 1, :]
            cnt = jnp.zeros_like(s1)
            for q in range(PEER_TOPK):
                cnt = cnt + jnp.where(s1 + t2_ref[q, hd:hd + 1, :] >= tau_h, 1.0, 0.0)
            a = jnp.exp(s1 - t1_ref[0, hd:hd + 1, :]) * inv_z[hd:hd + 1, :]
            c_ref[hd] = cnt.reshape(PEER_N_KEYS // 8, 8, tm)
            a_ref[hd] = a.reshape(PEER_N_KEYS // 8, 8, tm)
        acc_ref[...] = jnp.zeros_like(acc_ref)

    act = lax.dot_general(u_ref[...], hb_ref[...], NT_DIMS, preferred_element_type=F32)
    act = jax.nn.gelu(act)
    pieces = []
    for il in range(slabs):
        wt = jnp.zeros((PEER_N_KEYS, tm), BF16)
        for hd in range(PEER_HEADS):
            crow = c_ref[hd, e * (slabs // 8) + il // 8, il % 8:il % 8 + 1, :].astype(BF16)
            arow = a_ref[hd, e * (slabs // 8) + il // 8, il % 8:il % 8 + 1, :].astype(BF16)
            wt = wt + jnp.where(r2_ref[hd] < crow, arow * b_ref[hd], jnp.zeros((), BF16))
        g = act[il * PEER_N_KEYS:(il + 1) * PEER_N_KEYS, :]
        pieces.append((wt.astype(F32) * g).astype(BF16))
    p = jnp.concatenate(pieces, axis=0)
    acc_ref[...] += jnp.dot(vt_ref[...], p, preferred_element_type=F32)

    @pl.when(e == n_chunks - 1)
    def _finish():
        o_ref[...] = x_ref[...] + acc_ref[...].T


def _peer(x2, gain, wq_t, keys0, keys1, u, v_t, tm=512, ec=1024):
    n = x2.shape[0]
    n_exp = u.shape[0]
    assert ec % (8 * PEER_N_KEYS) == 0
    kernel = functools.partial(_peer_kernel, tm=tm, ec=ec)
    head_rows = pltpu.VMEM((PEER_HEADS, PEER_N_KEYS, tm), F32)
    return pl.pallas_call(
        kernel,
        grid=(n // tm, n_exp // ec),
        in_specs=[
            pl.BlockSpec((tm, D_MODEL), lambda i, e: (i, 0)),
            pl.BlockSpec((1, D_MODEL), lambda i, e: (0, 0)),
            pl.BlockSpec((PEER_HEADS * PEER_QUERY_DIM, D_MODEL), lambda i, e: (0, 0)),
            pl.BlockSpec((PEER_N_KEYS, PEER_HALF), lambda i, e: (0, 0)),
            pl.BlockSpec((PEER_N_KEYS, PEER_HALF), lambda i, e: (0, 0)),
            pl.BlockSpec((ec, D_MODEL), lambda i, e: (e, 0)),
            pl.BlockSpec((D_MODEL, ec), lambda i, e: (0, e)),
        ],
        out_specs=pl.BlockSpec((tm, D_MODEL), lambda i, e: (i, 0)),
        out_shape=jax.ShapeDtypeStruct((n, D_MODEL), F32),
        scratch_shapes=[
            pltpu.VMEM((tm, D_MODEL), BF16),
            head_rows,
            pltpu.VMEM((PEER_HEADS, PEER_N_KEYS, tm), BF16),
            pltpu.VMEM((PEER_HEADS, PEER_N_KEYS, tm), BF16),
            pltpu.VMEM((PEER_HEADS, PEER_N_KEYS // 8, 8, tm), F32),
            pltpu.VMEM((PEER_HEADS, PEER_N_KEYS // 8, 8, tm), F32),
            pltpu.VMEM((PEER_TOPK, PEER_HEADS, tm), F32),
            pltpu.VMEM((PEER_TOPK, PEER_HEADS, tm), F32),
            pltpu.VMEM((D_MODEL, tm), F32),
        ],
        compiler_params=_params(("parallel", "arbitrary")),
        name="peer",
    )(x2, gain.reshape(1, D_MODEL), wq_t, keys0, keys1, u, v_t)


def _final_norm_kernel(x_ref, g_ref, o_ref):
    o_ref[...] = _rms(x_ref[...], g_ref[...])


def _final_norm(x2, gain, tm=1024):
    n = x2.shape[0]
    return pl.pallas_call(
        _final_norm_kernel,
        grid=(n // tm,),
        in_specs=[pl.BlockSpec((tm, D_MODEL), lambda i: (i, 0)),
                  pl.BlockSpec((1, D_MODEL), lambda i: (0, 0))],
        out_specs=pl.BlockSpec((tm, D_MODEL), lambda i: (i, 0)),
        out_shape=jax.ShapeDtypeStruct((n, D_MODEL), F32),
        compiler_params=_params(("parallel",)),
        name="final_norm",
    )(x2, gain.reshape(1, D_MODEL))


def kernel(x, mem, norm_mix, w_in, ret_decay_logit, ret_norm_gain, att_norm_gain, w_out,
           norm_mem, norm_mem_kv, w_mem_q, w_mem_kv, w_mem_o, norm_ffn,
           peer_w_query, peer_sub_keys, peer_expert_down, peer_expert_up, norm_final):
    batch, seq, _ = x.shape
    mem_len = mem.shape[1]
    depth = w_in.shape[0]
    n = batch * seq
    x2 = x.reshape(n, D_MODEL)
    mem2 = mem.reshape(batch * mem_len, D_MODEL)
    for layer in range(depth):
        w_in_l = w_in[layer].astype(BF16)
        proj_ret, proj_att = _in_proj(x2, norm_mix[layer], w_in_l[:, :RET_PROJ], w_in_l[:, RET_PROJ:])
        ret_out = _retention(proj_ret.reshape(batch, seq, RET_PROJ), ret_decay_logit[layer],
                             ret_norm_gain[layer], batch, seq)
        att_out = _dilated_attention(proj_att.reshape(batch, seq, ATT_PROJ), att_norm_gain[layer],
                                     batch, seq)
        w_out_l = w_out[layer].astype(BF16)
        x2 = _out_proj(x2, ret_out.reshape(n, RET_WIDTH), att_out.reshape(n, ATT_WIDTH),
                       w_out_l[:RET_WIDTH], w_out_l[RET_WIDTH:])
        kv = _mem_kv(mem2, norm_mem_kv[layer], w_mem_kv[layer].astype(BF16))
        x3 = _mem_attn(x2.reshape(batch, seq, D_MODEL), norm_mem[layer], w_mem_q[layer].astype(BF16),
                       kv.reshape(batch, mem_len, 2 * D_MODEL), w_mem_o[layer].astype(BF16))
        x2 = x3.reshape(n, D_MODEL)
        x2 = _peer(x2, norm_ffn[layer],
                   peer_w_query[layer].T.astype(BF16),
                   peer_sub_keys[layer, 0].astype(BF16), peer_sub_keys[layer, 1].astype(BF16),
                   peer_expert_down[layer].astype(BF16),
                   peer_expert_up[layer].T.astype(BF16))
    out = _final_norm(x2, norm_final)
    return out.reshape(batch, seq, D_MODEL)
```

```python
import functools
import math

import jax
import jax.numpy as jnp
from jax import lax
from jax.experimental import pallas as pl
from jax.experimental.pallas import tpu as pltpu

F32 = jnp.float32
BF16 = jnp.bfloat16

D_MODEL = 1024
HEAD_DIM = 64
N_RET_HEADS = 8
N_ATT_HEADS = 8
RET_WIDTH = N_RET_HEADS * HEAD_DIM
ATT_WIDTH = N_ATT_HEADS * HEAD_DIM
RET_PROJ = 4 * RET_WIDTH
ATT_PROJ = 3 * ATT_WIDTH
RET_CHUNK = 128
DILATIONS = (1, 4, 16)
HALF_SPAN = 64
N_MEM_HEADS = 4
MEM_HEAD_DIM = D_MODEL // N_MEM_HEADS
PEER_HEADS = 8
PEER_N_KEYS = 128
PEER_TOPK = 16
PEER_QUERY_DIM = 256
PEER_HALF = PEER_QUERY_DIM // 2
RMS_EPS = 1e-6
NEG_INF = -1e30
LANES = 128
ATT_Q_BLOCK = 128
ATT_K_BLOCK = ATT_Q_BLOCK + 2 * HALF_SPAN
VMEM_LIMIT = 56 * 1024 * 1024

NT_DIMS = (((1,), (1,)), ((), ()))


def _params(sem):
    return pltpu.CompilerParams(dimension_semantics=sem, vmem_limit_bytes=VMEM_LIMIT)


def _rms(x, gain):
    ms = jnp.mean(x * x, axis=-1, keepdims=True)
    return x * lax.rsqrt(ms + RMS_EPS) * gain


def _pair_head_norm(y, gain, lane_lo):
    y2 = y * y
    ms0 = jnp.sum(jnp.where(lane_lo, y2, 0.0), axis=-1, keepdims=True)
    ms1 = jnp.sum(jnp.where(lane_lo, 0.0, y2), axis=-1, keepdims=True)
    ms = jnp.where(lane_lo, ms0, ms1) * (1.0 / HEAD_DIM)
    return y * lax.rsqrt(ms + RMS_EPS) * gain


def _in_proj_kernel(x_ref, g_ref, wr_ref, wa_ref, or_ref, oa_ref):
    h = _rms(x_ref[...], g_ref[...]).astype(BF16)
    or_ref[...] = jnp.dot(h, wr_ref[...], preferred_element_type=F32).astype(or_ref.dtype)
    oa_ref[...] = jnp.dot(h, wa_ref[...], preferred_element_type=F32).astype(oa_ref.dtype)


def _in_proj(x2, gain, w_ret, w_att, tm=512):
    n = x2.shape[0]
    return pl.pallas_call(
        _in_proj_kernel,
        grid=(n // tm,),
        in_specs=[
            pl.BlockSpec((tm, D_MODEL), lambda i: (i, 0)),
            pl.BlockSpec((1, D_MODEL), lambda i: (0, 0)),
            pl.BlockSpec((D_MODEL, RET_PROJ), lambda i: (0, 0)),
            pl.BlockSpec((D_MODEL, ATT_PROJ), lambda i: (0, 0)),
        ],
        out_specs=[
            pl.BlockSpec((tm, RET_PROJ), lambda i: (i, 0)),
            pl.BlockSpec((tm, ATT_PROJ), lambda i: (i, 0)),
        ],
        out_shape=[
            jax.ShapeDtypeStruct((n, RET_PROJ), BF16),
            jax.ShapeDtypeStruct((n, ATT_PROJ), F32),
        ],
        compiler_params=_params(("parallel",)),
        name="in_proj",
    )(x2, gain.reshape(1, D_MODEL), w_ret, w_att)


def _ret_kernel(lg_ref, gain_ref, q_ref, k_ref, v_ref, g_ref, o_ref, kv_ref, st_ref, *, seq):
    c_len = RET_CHUNK
    n_chunks = seq // c_len
    lg = jax.nn.log_sigmoid(lg_ref[0])
    lgf = lg[0:1, :]
    lgb = lg[1:2, :]
    row = lax.broadcasted_iota(jnp.int32, (c_len, LANES), 0)
    lane = lax.broadcasted_iota(jnp.int32, (c_len, LANES), 1)
    lane_lo = lane < HEAD_DIM
    block_diag = (row < HEAD_DIM) == lane_lo
    rowf = row.astype(F32)
    rel = (row - lane).astype(F32)

    def decay_matrix(col):
        f = jnp.exp(lgf[:, col:col + 1] * jnp.maximum(rel, 0.0))
        b = jnp.exp(lgb[:, col:col + 1] * jnp.maximum(-rel, 0.0))
        return jnp.where(rel >= 0, f, b)

    decay2 = jnp.concatenate([decay_matrix(0), decay_matrix(HEAD_DIM)], axis=0)
    xi_f = jnp.exp(lgf * (rowf + 1.0))
    zeta_f = jnp.exp(lgf * (c_len - 1.0 - rowf))
    cd_f = jnp.exp(lgf * float(c_len))
    xi_b = jnp.exp(lgb * (c_len - rowf))
    zeta_b = jnp.exp(lgb * rowf)
    cd_b = jnp.exp(lgb * float(c_len))

    zeta2 = jnp.concatenate([zeta_f, zeta_b], axis=1)
    block_diag2 = jnp.concatenate([block_diag, block_diag], axis=0)

    def summary_step(c, carry):
        sl = pl.ds(pl.multiple_of(c * c_len, c_len), c_len)
        ks = k_ref[sl, :].astype(F32) * (HEAD_DIM ** -0.5)
        kz_t = (jnp.concatenate([ks, ks], axis=1) * zeta2).T.astype(BF16)
        upd = jnp.dot(kz_t, v_ref[sl, :], preferred_element_type=F32)
        kv_ref[c] = jnp.where(block_diag2, upd, 0.0)
        return carry

    lax.fori_loop(0, n_chunks, summary_step, 0, unroll=4)

    state = jnp.zeros((LANES, LANES), F32)
    for c in range(n_chunks):
        st_ref[c, :, 0:LANES] = state.astype(BF16)
        state = state * cd_f + kv_ref[c, 0:LANES, :]
    state = jnp.zeros((LANES, LANES), F32)
    for c in reversed(range(n_chunks)):
        st_ref[c, :, LANES:2 * LANES] = state.astype(BF16)
        state = state * cd_b + kv_ref[c, LANES:2 * LANES, :]

    gain = gain_ref[...]

    def out_step(c, carry):
        sl = pl.ds(pl.multiple_of(c * c_len, c_len), c_len)
        qc = q_ref[sl, :]
        vc = v_ref[sl, :]
        ks = k_ref[sl, :].astype(F32) * (HEAD_DIM ** -0.5)
        q2 = jnp.concatenate([jnp.where(lane_lo, qc, jnp.zeros_like(qc)),
                              jnp.where(lane_lo, jnp.zeros_like(qc), qc)], axis=0)
        s = lax.dot_general(q2, ks.astype(BF16), NT_DIMS, preferred_element_type=F32)
        p = (s * decay2).astype(BF16)
        o2 = jnp.dot(p, vc, preferred_element_type=F32)
        inter = jnp.dot(qc, st_ref[c], preferred_element_type=F32)
        y = (jnp.where(lane_lo, o2[:c_len], o2[c_len:])
             + inter[:, 0:LANES] * xi_f + inter[:, LANES:2 * LANES] * xi_b)
        yn = _pair_head_norm(y, gain, lane_lo)
        o_ref[sl, :] = (yn * jax.nn.silu(g_ref[sl, :].astype(F32))).astype(o_ref.dtype)
        return carry

    lax.fori_loop(0, n_chunks, out_step, 0, unroll=4)


def _retention(proj_ret, decay_logit, gain, batch, seq):
    n_pairs = N_RET_HEADS // 2
    lg = jnp.repeat(decay_logit.astype(F32), HEAD_DIM, axis=-1)
    lg = lg.reshape(2, n_pairs, LANES).transpose(1, 0, 2)
    col = lambda off: (lambda b, p: (b, 0, off + p))
    blk = lambda off: pl.BlockSpec((None, seq, LANES), col(off))
    return pl.pallas_call(
        functools.partial(_ret_kernel, seq=seq),
        grid=(batch, n_pairs),
        in_specs=[
            pl.BlockSpec((1, 2, LANES), lambda b, p: (p, 0, 0)),
            pl.BlockSpec((1, LANES), lambda b, p: (0, p)),
            blk(0), blk(n_pairs), blk(2 * n_pairs), blk(3 * n_pairs),
        ],
        out_specs=pl.BlockSpec((None, seq, LANES), lambda b, p: (b, 0, p)),
        out_shape=jax.ShapeDtypeStruct((batch, seq, RET_WIDTH), BF16),
        scratch_shapes=[
            pltpu.VMEM((seq // RET_CHUNK, 2 * LANES, LANES), F32),
            pltpu.VMEM((seq // RET_CHUNK, LANES, 2 * LANES), BF16),
        ],
        compiler_params=_params(("parallel", "parallel")),
        name="retention",
    )(lg, gain.reshape(1, RET_WIDTH), proj_ret, proj_ret, proj_ret, proj_ret)


def _att_kernel(gain_ref, q_ref, k_ref, v_ref, o_ref,
                qd_ref, kd_ref, vd_ref, od_ref, ld_ref, op_ref, lp_ref, *, seq):
    qb_len = ATT_Q_BLOCK
    kb_len = ATT_K_BLOCK
    n_blocks = seq // qb_len
    pair = pl.program_id(1)
    lane = lax.broadcasted_iota(jnp.int32, (qb_len, LANES), 1)
    lane_lo = lane < HEAD_DIM

    srow = lax.broadcasted_iota(jnp.int32, (2 * qb_len, kb_len), 0)
    scol = lax.broadcasted_iota(jnp.int32, (2 * qb_len, kb_len), 1)
    head_in_pair = (srow >= qb_len).astype(F32)
    qrow = jnp.where(srow >= qb_len, srow - qb_len, srow)
    dist = jnp.abs(scol - HALF_SPAN - qrow)
    in_band = dist <= HALF_SPAN
    slope = jnp.exp(-math.log(2.0) * (2.0 * pair.astype(F32) + 1.0 + head_in_pair))
    base_bias = -slope * dist.astype(F32)
    kcol = lax.broadcasted_iota(jnp.int32, (1, kb_len), 1)

    zeros_pad = jnp.zeros((HALF_SPAN, LANES), F32)

    for pat, dil in enumerate(DILATIONS):
        sub_len = seq // dil
        cls_len = sub_len + 2 * HALF_SPAN
        blocks_per_cls = sub_len // qb_len
        bias = jnp.where(in_band, base_bias * float(dil), NEG_INF)

        for r in range(dil):
            kd_ref[r * cls_len:r * cls_len + HALF_SPAN, :] = zeros_pad
            vd_ref[r * cls_len:r * cls_len + HALF_SPAN, :] = zeros_pad
            kd_ref[r * cls_len + HALF_SPAN + sub_len:(r + 1) * cls_len, :] = zeros_pad
            vd_ref[r * cls_len + HALF_SPAN + sub_len:(r + 1) * cls_len, :] = zeros_pad
            for j in range(blocks_per_cls):
                if dil == 1:
                    src = pl.ds(j * qb_len, qb_len)
                else:
                    src = pl.ds(r + j * qb_len * dil, qb_len, stride=dil)
                dst_q = pl.ds(r * sub_len + j * qb_len, qb_len)
                dst_k = pl.ds(r * cls_len + HALF_SPAN + j * qb_len, qb_len)
                qd_ref[dst_q, :] = q_ref[src, :]
                kd_ref[dst_k, :] = k_ref[src, :]
                vd_ref[dst_k, :] = v_ref[src, :]

        def block_step(i, carry, cls_len=cls_len, sub_len=sub_len,
                       blocks_per_cls=blocks_per_cls, bias=bias):
            r = i // blocks_per_cls
            m = i - r * blocks_per_cls
            q_off = pl.multiple_of(i * qb_len, qb_len)
            k_off = pl.multiple_of(r * cls_len + m * qb_len, HALF_SPAN)
            qb = qd_ref[pl.ds(q_off, qb_len), :] * (HEAD_DIM ** -0.5)
            q2 = jnp.concatenate([jnp.where(lane_lo, qb, 0.0),
                                  jnp.where(lane_lo, 0.0, qb)], axis=0).astype(BF16)
            kw = kd_ref[pl.ds(k_off, kb_len), :].astype(BF16)
            vw = vd_ref[pl.ds(k_off, kb_len), :].astype(BF16)
            s = lax.dot_general(q2, kw, NT_DIMS, preferred_element_type=F32)
            kpos = kcol + (m * qb_len - HALF_SPAN)
            valid = (kpos >= 0) & (kpos < sub_len)
            s = jnp.where(valid, s + bias, NEG_INF)
            mx = jnp.max(s, axis=-1, keepdims=True)
            p = jnp.exp(s - mx)
            l = jnp.sum(p, axis=-1, keepdims=True)
            o2 = jnp.dot(p.astype(BF16), vw, preferred_element_type=F32) / l
            lse = mx + jnp.log(l)
            od_ref[pl.ds(q_off, qb_len), :] = jnp.where(lane_lo, o2[:qb_len], o2[qb_len:])
            ld_ref[pl.ds(q_off, qb_len), :] = jnp.where(
                lane_lo, jnp.broadcast_to(lse[:qb_len], (qb_len, LANES)),
                jnp.broadcast_to(lse[qb_len:], (qb_len, LANES)))
            return carry

        lax.fori_loop(0, n_blocks, block_step, 0, unroll=8)

        for r in range(dil):
            for j in range(blocks_per_cls):
                src = pl.ds(r * sub_len + j * qb_len, qb_len)
                if dil == 1:
                    dst = pl.ds(j * qb_len, qb_len)
                else:
                    dst = pl.ds(r + j * qb_len * dil, qb_len, stride=dil)
                op_ref[pat, dst, :] = od_ref[src, :]
                lp_ref[pat, dst, :] = ld_ref[src, :]

    gain = gain_ref[...]

    def mix_step(i, carry):
        sl = pl.ds(pl.multiple_of(i * qb_len, qb_len), qb_len)
        l0 = lp_ref[0, sl, :]
        l1 = lp_ref[1, sl, :]
        l2 = lp_ref[2, sl, :]
        mm = jnp.maximum(jnp.maximum(l0, l1), l2)
        w0 = jnp.exp(l0 - mm)
        w1 = jnp.exp(l1 - mm)
        w2 = jnp.exp(l2 - mm)
        y = (w0 * op_ref[0, sl, :] + w1 * op_ref[1, sl, :] + w2 * op_ref[2, sl, :]) / (w0 + w1 + w2)
        o_ref[sl, :] = _pair_head_norm(y, gain, lane_lo).astype(o_ref.dtype)
        return carry

    lax.fori_loop(0, n_blocks, mix_step, 0, unroll=4)


def _dilated_attention(proj_att, gain, batch, seq):
    n_pairs = N_ATT_HEADS // 2
    max_cls_rows = max(d * (seq // d + 2 * HALF_SPAN) for d in DILATIONS)
    blk = lambda off: pl.BlockSpec((None, seq, LANES), lambda b, p: (b, 0, off + p))
    return pl.pallas_call(
        functools.partial(_att_kernel, seq=seq),
        grid=(batch, n_pairs),
        in_specs=[
            pl.BlockSpec((1, LANES), lambda b, p: (0, p)),
            blk(0), blk(n_pairs), blk(2 * n_pairs),
        ],
        out_specs=pl.BlockSpec((None, seq, LANES), lambda b, p: (b, 0, p)),
        out_shape=jax.ShapeDtypeStruct((batch, seq, ATT_WIDTH), BF16),
        scratch_shapes=[
            pltpu.VMEM((seq, LANES), F32),
            pltpu.VMEM((max_cls_rows, LANES), F32),
            pltpu.VMEM((max_cls_rows, LANES), F32),
            pltpu.VMEM((seq, LANES), F32),
            pltpu.VMEM((seq, LANES), F32),
            pltpu.VMEM((len(DILATIONS), seq, LANES), F32),
            pltpu.VMEM((len(DILATIONS), seq, LANES), F32),
        ],
        compiler_params=_params(("parallel", "parallel")),
        name="dilated_attention",
    )(gain.reshape(1, ATT_WIDTH), proj_att, proj_att, proj_att)


def _out_proj_kernel(x_ref, a_ref, b_ref, wa_ref, wb_ref, o_ref):
    y = jnp.dot(a_ref[...], wa_ref[...], preferred_element_type=F32)
    y = y + jnp.dot(b_ref[...], wb_ref[...], preferred_element_type=F32)
    o_ref[...] = x_ref[...] + y


def _out_proj(x2, ret_out, att_out, w_a, w_b, tm=512):
    n = x2.shape[0]
    return pl.pallas_call(
        _out_proj_kernel,
        grid=(n // tm,),
        in_specs=[
            pl.BlockSpec((tm, D_MODEL), lambda i: (i, 0)),
            pl.BlockSpec((tm, RET_WIDTH), lambda i: (i, 0)),
            pl.BlockSpec((tm, ATT_WIDTH), lambda i: (i, 0)),
            pl.BlockSpec((RET_WIDTH, D_MODEL), lambda i: (0, 0)),
            pl.BlockSpec((ATT_WIDTH, D_MODEL), lambda i: (0, 0)),
        ],
        out_specs=pl.BlockSpec((tm, D_MODEL), lambda i: (i, 0)),
        out_shape=jax.ShapeDtypeStruct((n, D_MODEL), F32),
        compiler_params=_params(("parallel",)),
        name="out_proj",
    )(x2, ret_out, att_out, w_a, w_b)


def _mem_kv_kernel(m_ref, g_ref, w_ref, o_ref):
    h = _rms(m_ref[...], g_ref[...]).astype(BF16)
    o_ref[...] = jnp.dot(h, w_ref[...], preferred_element_type=F32).astype(o_ref.dtype)


def _mem_kv(mem2, gain, w_kv, tm=512):
    n = mem2.shape[0]
    return pl.pallas_call(
        _mem_kv_kernel,
        grid=(n // tm,),
        in_specs=[
            pl.BlockSpec((tm, D_MODEL), lambda i: (i, 0)),
            pl.BlockSpec((1, D_MODEL), lambda i: (0, 0)),
            pl.BlockSpec((D_MODEL, 2 * D_MODEL), lambda i: (0, 0)),
        ],
        out_specs=pl.BlockSpec((tm, 2 * D_MODEL), lambda i: (i, 0)),
        out_shape=jax.ShapeDtypeStruct((n, 2 * D_MODEL), BF16),
        compiler_params=_params(("parallel",)),
        name="mem_kv",
    )(mem2, gain.reshape(1, D_MODEL), w_kv)


def _mem_attn_kernel(x_ref, g_ref, wq_ref, kv_ref, wo_ref, o_ref):
    x = x_ref[...]
    h = _rms(x, g_ref[...]).astype(BF16)
    q = jnp.dot(h, wq_ref[...], preferred_element_type=F32) * (MEM_HEAD_DIM ** -0.5)
    q = q.astype(BF16)
    outs = []
    for hd in range(N_MEM_HEADS):
        lo = hd * MEM_HEAD_DIM
        kh = kv_ref[:, lo:lo + MEM_HEAD_DIM]
        vh = kv_ref[:, D_MODEL + lo:D_MODEL + lo + MEM_HEAD_DIM]
        s = lax.dot_general(q[:, lo:lo + MEM_HEAD_DIM], kh, NT_DIMS, preferred_element_type=F32)
        mx = jnp.max(s, axis=-1, keepdims=True)
        p = jnp.exp(s - mx)
        p = p / jnp.sum(p, axis=-1, keepdims=True)
        outs.append(jnp.dot(p.astype(BF16), vh, preferred_element_type=F32).astype(BF16))
    o = jnp.concatenate(outs, axis=-1)
    o_ref[...] = x + jnp.dot(o, wo_ref[...], preferred_element_type=F32)


def _mem_attn(x3, gain, w_q, kv3, w_o, tm=512):
    batch, seq, _ = x3.shape
    mem_len = kv3.shape[1]
    return pl.pallas_call(
        _mem_attn_kernel,
        grid=(batch, seq // tm),
        in_specs=[
            pl.BlockSpec((None, tm, D_MODEL), lambda b, i: (b, i, 0)),
            pl.BlockSpec((1, D_MODEL), lambda b, i: (0, 0)),
            pl.BlockSpec((D_MODEL, D_MODEL), lambda b, i: (0, 0)),
            pl.BlockSpec((None, mem_len, 2 * D_MODEL), lambda b, i: (b, 0, 0)),
            pl.BlockSpec((D_MODEL, D_MODEL), lambda b, i: (0, 0)),
        ],
        out_specs=pl.BlockSpec((None, tm, D_MODEL), lambda b, i: (b, i, 0)),
        out_shape=jax.ShapeDtypeStruct((batch, seq, D_MODEL), F32),
        compiler_params=_params(("parallel", "parallel")),
        name="mem_attn",
    )(x3, gain.reshape(1, D_MODEL), w_q, kv3, w_o)


def _candidate_pairs():
    return [(p, q) for p in range(PEER_TOPK) for q in range(PEER_TOPK)
            if (p + 1) * (q + 1) <= PEER_TOPK]


def _gelu_tanh(x):
    k0 = -2.0 * math.sqrt(2.0 / math.pi) * math.log2(math.e)
    return x / (1.0 + jnp.exp2(x * (k0 + (k0 * 0.044715) * (x * x))))


def _twin_bf16_words(v):
    bits = lax.bitcast_convert_type(v.astype(BF16).astype(F32), jnp.uint32)
    return bits | (bits >> 16)


def _peer_kernel(x_ref, g_ref, wqt_ref, k0_ref, k1_ref, u_ref, vt_ref, o_ref,
                 ht_ref, s1_ref, s2_ref, r2_ref, b_ref, c_ref, a_ref, t1_ref, t2_ref,
                 cc_ref, invz_ref, acc_ref, *, tm, ec, sub):
    e = pl.program_id(1)
    n_chunks = pl.num_programs(1)
    slabs = ec // PEER_N_KEYS
    n_lane_blocks = tm // LANES
    lowest = float(jnp.finfo(jnp.float32).min)

    zero = jnp.zeros((), BF16)

    def row_tile(words):
        return pltpu.bitcast(jnp.broadcast_to(words, (PEER_N_KEYS // 2, tm)), BF16)

    def gate_slab(il):
        grp = e * (slabs // 8) + il // 8
        wt = None
        for hd in range(PEER_HEADS):
            crow = row_tile(c_ref[hd, grp, il % 8:il % 8 + 1, :])
            arow = row_tile(a_ref[hd, grp, il % 8:il % 8 + 1, :])
            term = arow * jnp.where(r2_ref[hd] < crow, b_ref[hd], zero)
            wt = term if wt is None else wt + term
        return wt

    @pl.when(e == 0)
    def _select():
        ht_ref[...] = _rms(x_ref[...], g_ref[...]).T.astype(BF16)

        def scores(hd, carry):
            rows = pl.ds(pl.multiple_of(hd * PEER_QUERY_DIM, PEER_QUERY_DIM), PEER_QUERY_DIM)
            qt = jnp.dot(wqt_ref[rows, :], ht_ref[...], preferred_element_type=F32).astype(BF16)
            s1_ref[hd] = jnp.dot(k0_ref[...], qt[:PEER_HALF], preferred_element_type=F32)
            s2_ref[hd] = jnp.dot(k1_ref[...], qt[PEER_HALF:], preferred_element_type=F32)
            return carry

        lax.fori_loop(0, PEER_HEADS, scores, 0)

        for hd in range(PEER_HEADS):
            def extract(lb, carry, hd=hd):
                lanes = pl.ds(pl.multiple_of(lb * LANES, LANES), LANES)
                w = s1_ref[hd, :, lanes]
                for it in range(PEER_TOPK):
                    mx = jnp.max(w, axis=0, keepdims=True)
                    t1_ref[it, hd:hd + 1, lanes] = mx
                    if it + 1 < PEER_TOPK:
                        w = jnp.where(w == mx, lowest, w)
                s2 = s2_ref[hd, :, lanes]
                w = s2
                rank = jnp.full(s2.shape, float(PEER_N_KEYS), F32)
                top2 = None
                for it in range(PEER_TOPK):
                    mx = jnp.max(w, axis=0, keepdims=True)
                    top2 = mx if top2 is None else top2
                    t2_ref[it, hd:hd + 1, lanes] = mx
                    hit = w == mx
                    rank = jnp.where(hit, float(it), rank)
                    if it + 1 < PEER_TOPK:
                        w = jnp.where(hit, lowest, w)
                r2_ref[hd, :, lanes] = rank.astype(BF16)
                b_ref[hd, :, lanes] = jnp.exp(s2 - top2).astype(BF16)
                return carry

            lax.fori_loop(0, n_lane_blocks, extract, 0)

        def compact(lb, carry):
            lanes = pl.ds(pl.multiple_of(lb * LANES, LANES), LANES)
            t1 = [t1_ref[p, :, lanes] for p in range(PEER_TOPK)]
            t2 = [t2_ref[q, :, lanes] for q in range(PEER_TOPK)]
            top = t1[0] + t2[0]
            cands = [t1[p] + t2[q] for p, q in _candidate_pairs()]
            z = jnp.zeros_like(top)
            tau = top
            for it in range(PEER_TOPK):
                mx = cands[0]
                for cnd in cands[1:]:
                    mx = jnp.maximum(mx, cnd)
                z = z + jnp.exp(mx - top)
                tau = mx
                if it + 1 < PEER_TOPK:
                    cands = [jnp.where(cnd == mx, lowest, cnd) for cnd in cands]
            invz_ref[:, lanes] = 1.0 / z
            for p in range(PEER_TOPK):
                cc = jnp.zeros_like(top)
                for q in range(PEER_TOPK // (p + 1)):
                    cc = cc + jnp.where(t1[p] + t2[q] >= tau, 1.0, 0.0)
                cc_ref[p, :, lanes] = cc
            return carry

        lax.fori_loop(0, n_lane_blocks, compact, 0)

        for hd in range(PEER_HEADS):
            def rows(lb, carry, hd=hd):
                lanes = pl.ds(pl.multiple_of(lb * LANES, LANES), LANES)
                s1 = s1_ref[hd, :, lanes]
                cnt = jnp.zeros_like(s1)
                for p in reversed(range(PEER_TOPK)):
                    cnt = jnp.where(s1 >= t1_ref[p, hd:hd + 1, lanes], cc_ref[p, hd:hd + 1, lanes], cnt)
                a = jnp.exp(s1 - t1_ref[0, hd:hd + 1, lanes]) * invz_ref[hd:hd + 1, lanes]
                c_ref[hd, :, :, lanes] = _twin_bf16_words(cnt).reshape(PEER_N_KEYS // 8, 8, LANES)
                a_ref[hd, :, :, lanes] = _twin_bf16_words(a).reshape(PEER_N_KEYS // 8, 8, LANES)
                return carry

            lax.fori_loop(0, n_lane_blocks, rows, 0)
        acc_ref[...] = jnp.zeros_like(acc_ref)

    sub_slabs = sub // PEER_N_KEYS
    pieces = []
    for s in range(ec // sub):
        act = jnp.dot(u_ref[s * sub:(s + 1) * sub, :], ht_ref[...], preferred_element_type=F32)
        g = _gelu_tanh(act).astype(BF16)
        for k in range(sub_slabs):
            pieces.append(gate_slab(s * sub_slabs + k) * g[k * PEER_N_KEYS:(k + 1) * PEER_N_KEYS, :])
    p = jnp.concatenate(pieces, axis=0)
    acc_ref[...] += jnp.dot(vt_ref[...], p, preferred_element_type=F32)

    @pl.when(e == n_chunks - 1)
    def _finish():
        o_ref[...] = x_ref[...] + acc_ref[...].T


def _peer(x2, gain, wq_t, keys0, keys1, u, v_t, tm=512, ec=2048, sub=128):
    n = x2.shape[0]
    n_chunks = u.shape[0] // ec
    assert ec % (8 * PEER_N_KEYS) == 0 and ec % sub == 0 and sub % PEER_N_KEYS == 0
    kernel = functools.partial(_peer_kernel, tm=tm, ec=ec, sub=sub)
    head_rows = pltpu.VMEM((PEER_HEADS, PEER_N_KEYS, tm), F32)
    rank_rows = pltpu.VMEM((PEER_TOPK, PEER_HEADS, tm), F32)
    return pl.pallas_call(
        kernel,
        grid=(n // tm, n_chunks),
        in_specs=[
            pl.BlockSpec((tm, D_MODEL), lambda i, e: (i, 0)),
            pl.BlockSpec((1, D_MODEL), lambda i, e: (0, 0)),
            pl.BlockSpec((PEER_HEADS * PEER_QUERY_DIM, D_MODEL), lambda i, e: (0, 0)),
            pl.BlockSpec((PEER_N_KEYS, PEER_HALF), lambda i, e: (0, 0)),
            pl.BlockSpec((PEER_N_KEYS, PEER_HALF), lambda i, e: (0, 0)),
            pl.BlockSpec((ec, D_MODEL), lambda i, e: (e, 0)),
            pl.BlockSpec((D_MODEL, ec), lambda i, e: (0, e)),
        ],
        out_specs=pl.BlockSpec((tm, D_MODEL), lambda i, e: (i, 0)),
        out_shape=jax.ShapeDtypeStruct((n, D_MODEL), F32),
        scratch_shapes=[
            pltpu.VMEM((D_MODEL, tm), BF16),
            head_rows,
            head_rows,
            pltpu.VMEM((PEER_HEADS, PEER_N_KEYS, tm), BF16),
            pltpu.VMEM((PEER_HEADS, PEER_N_KEYS, tm), BF16),
            pltpu.VMEM((PEER_HEADS, PEER_N_KEYS // 8, 8, tm), jnp.uint32),
            pltpu.VMEM((PEER_HEADS, PEER_N_KEYS // 8, 8, tm), jnp.uint32),
            rank_rows,
            rank_rows,
            rank_rows,
            pltpu.VMEM((PEER_HEADS, tm), F32),
            pltpu.VMEM((D_MODEL, tm), F32),
        ],
        compiler_params=_params(("parallel", "arbitrary")),
        name="peer",
    )(x2, gain.reshape(1, D_MODEL), wq_t, keys0, keys1, u, v_t)


def _final_norm_kernel(x_ref, g_ref, o_ref):
    o_ref[...] = _rms(x_ref[...], g_ref[...])


def _final_norm(x2, gain, tm=1024):
    n = x2.shape[0]
    return pl.pallas_call(
        _final_norm_kernel,
        grid=(n // tm,),
        in_specs=[pl.BlockSpec((tm, D_MODEL), lambda i: (i, 0)),
                  pl.BlockSpec((1, D_MODEL), lambda i: (0, 0))],
        out_specs=pl.BlockSpec((tm, D_MODEL), lambda i: (i, 0)),
        out_shape=jax.ShapeDtypeStruct((n, D_MODEL), F32),
        compiler_params=_params(("parallel",)),
        name="final_norm",
    )(x2, gain.reshape(1, D_MODEL))


def kernel(x, mem, norm_mix, w_in, ret_decay_logit, ret_norm_gain, att_norm_gain, w_out,
           norm_mem, norm_mem_kv, w_mem_q, w_mem_kv, w_mem_o, norm_ffn,
           peer_w_query, peer_sub_keys, peer_expert_down, peer_expert_up, norm_final):
    batch, seq, _ = x.shape
    mem_len = mem.shape[1]
    depth = w_in.shape[0]
    n = batch * seq
    x2 = x.reshape(n, D_MODEL)
    mem2 = mem.reshape(batch * mem_len, D_MODEL)
    for layer in range(depth):
        w_in_l = w_in[layer].astype(BF16)
        proj_ret, proj_att = _in_proj(x2, norm_mix[layer], w_in_l[:, :RET_PROJ], w_in_l[:, RET_PROJ:])
        ret_out = _retention(proj_ret.reshape(batch, seq, RET_PROJ), ret_decay_logit[layer],
                             ret_norm_gain[layer], batch, seq)
        att_out = _dilated_attention(proj_att.reshape(batch, seq, ATT_PROJ), att_norm_gain[layer],
                                     batch, seq)
        w_out_l = w_out[layer].astype(BF16)
        x2 = _out_proj(x2, ret_out.reshape(n, RET_WIDTH), att_out.reshape(n, ATT_WIDTH),
                       w_out_l[:RET_WIDTH], w_out_l[RET_WIDTH:])
        kv = _mem_kv(mem2, norm_mem_kv[layer], w_mem_kv[layer].astype(BF16))
        x3 = _mem_attn(x2.reshape(batch, seq, D_MODEL), norm_mem[layer], w_mem_q[layer].astype(BF16),
                       kv.reshape(batch, mem_len, 2 * D_MODEL), w_mem_o[layer].astype(BF16))
        x2 = x3.reshape(n, D_MODEL)
        x2 = _peer(x2, norm_ffn[layer],
                   peer_w_query[layer].T.astype(BF16),
                   peer_sub_keys[layer, 0].astype(BF16), peer_sub_keys[layer, 1].astype(BF16),
                   peer_expert_down[layer].astype(BF16),
                   peer_expert_up[layer].T.astype(BF16))
    out = _final_norm(x2, norm_final)
    return out.reshape(batch, seq, D_MODEL)
```

```python
import functools
import math

import jax
import jax.numpy as jnp
from jax import lax
from jax.experimental import pallas as pl
from jax.experimental.pallas import tpu as pltpu

F32 = jnp.float32
BF16 = jnp.bfloat16

D_MODEL = 1024
HEAD_DIM = 64
N_RET_HEADS = 8
N_ATT_HEADS = 8
RET_WIDTH = N_RET_HEADS * HEAD_DIM
ATT_WIDTH = N_ATT_HEADS * HEAD_DIM
RET_PROJ = 4 * RET_WIDTH
ATT_PROJ = 3 * ATT_WIDTH
RET_CHUNK = 128
DILATIONS = (1, 4, 16)
HALF_SPAN = 64
N_MEM_HEADS = 4
MEM_HEAD_DIM = D_MODEL // N_MEM_HEADS
PEER_HEADS = 8
PEER_N_KEYS = 128
PEER_TOPK = 16
PEER_QUERY_DIM = 256
PEER_HALF = PEER_QUERY_DIM // 2
RMS_EPS = 1e-6
NEG_INF = -1e30
LANES = 128
ATT_Q_BLOCK = 128
ATT_K_BLOCK = ATT_Q_BLOCK + 2 * HALF_SPAN
VMEM_LIMIT = 56 * 1024 * 1024

NT_DIMS = (((1,), (1,)), ((), ()))


def _params(sem):
    return pltpu.CompilerParams(dimension_semantics=sem, vmem_limit_bytes=VMEM_LIMIT)


def _rms(x, gain):
    ms = jnp.mean(x * x, axis=-1, keepdims=True)
    return x * lax.rsqrt(ms + RMS_EPS) * gain


def _pair_head_norm(y, gain, lane_lo):
    y2 = y * y
    ms0 = jnp.sum(jnp.where(lane_lo, y2, 0.0), axis=-1, keepdims=True)
    ms1 = jnp.sum(jnp.where(lane_lo, 0.0, y2), axis=-1, keepdims=True)
    ms = jnp.where(lane_lo, ms0, ms1) * (1.0 / HEAD_DIM)
    return y * lax.rsqrt(ms + RMS_EPS) * gain


def _in_proj_kernel(x_ref, g_ref, wr_ref, wa_ref, or_ref, oa_ref):
    h = _rms(x_ref[...], g_ref[...]).astype(BF16)
    or_ref[...] = jnp.dot(h, wr_ref[...], preferred_element_type=F32).astype(or_ref.dtype)
    oa_ref[...] = jnp.dot(h, wa_ref[...], preferred_element_type=F32).astype(oa_ref.dtype)


def _in_proj(x2, gain, w_ret, w_att, tm=512):
    n = x2.shape[0]
    return pl.pallas_call(
        _in_proj_kernel,
        grid=(n // tm,),
        in_specs=[
            pl.BlockSpec((tm, D_MODEL), lambda i: (i, 0)),
            pl.BlockSpec((1, D_MODEL), lambda i: (0, 0)),
            pl.BlockSpec((D_MODEL, RET_PROJ), lambda i: (0, 0)),
            pl.BlockSpec((D_MODEL, ATT_PROJ), lambda i: (0, 0)),
        ],
        out_specs=[
            pl.BlockSpec((tm, RET_PROJ), lambda i: (i, 0)),
            pl.BlockSpec((tm, ATT_PROJ), lambda i: (i, 0)),
        ],
        out_shape=[
            jax.ShapeDtypeStruct((n, RET_PROJ), BF16),
            jax.ShapeDtypeStruct((n, ATT_PROJ), F32),
        ],
        compiler_params=_params(("parallel",)),
        name="in_proj",
    )(x2, gain.reshape(1, D_MODEL), w_ret, w_att)


def _ret_kernel(lg_ref, gain_ref, q_ref, k_ref, v_ref, g_ref, o_ref, kv_ref, st_ref, *, seq):
    c_len = RET_CHUNK
    n_chunks = seq // c_len
    lg = jax.nn.log_sigmoid(lg_ref[0])
    lgf = lg[0:1, :]
    lgb = lg[1:2, :]
    row = lax.broadcasted_iota(jnp.int32, (c_len, LANES), 0)
    lane = lax.broadcasted_iota(jnp.int32, (c_len, LANES), 1)
    lane_lo = lane < HEAD_DIM
    block_diag = (row < HEAD_DIM) == lane_lo
    rowf = row.astype(F32)
    rel = (row - lane).astype(F32)

    def decay_matrix(col):
        f = jnp.exp(lgf[:, col:col + 1] * jnp.maximum(rel, 0.0))
        b = jnp.exp(lgb[:, col:col + 1] * jnp.maximum(-rel, 0.0))
        return jnp.where(rel >= 0, f, b)

    decay2 = jnp.concatenate([decay_matrix(0), decay_matrix(HEAD_DIM)], axis=0)
    xi_f = jnp.exp(lgf * (rowf + 1.0))
    zeta_f = jnp.exp(lgf * (c_len - 1.0 - rowf))
    cd_f = jnp.exp(lgf * float(c_len))
    xi_b = jnp.exp(lgb * (c_len - rowf))
    zeta_b = jnp.exp(lgb * rowf)
    cd_b = jnp.exp(lgb * float(c_len))

    zeta2 = jnp.concatenate([zeta_f, zeta_b], axis=1)
    block_diag2 = jnp.concatenate([block_diag, block_diag], axis=0)

    def summary_step(c, carry):
        sl = pl.ds(pl.multiple_of(c * c_len, c_len), c_len)
        ks = k_ref[sl, :].astype(F32) * (HEAD_DIM ** -0.5)
        kz_t = (jnp.concatenate([ks, ks], axis=1) * zeta2).T.astype(BF16)
        upd = jnp.dot(kz_t, v_ref[sl, :], preferred_element_type=F32)
        kv_ref[c] = jnp.where(block_diag2, upd, 0.0)
        return carry

    lax.fori_loop(0, n_chunks, summary_step, 0, unroll=4)

    state = jnp.zeros((LANES, LANES), F32)
    for c in range(n_chunks):
        st_ref[c, :, 0:LANES] = state.astype(BF16)
        state = state * cd_f + kv_ref[c, 0:LANES, :]
    state = jnp.zeros((LANES, LANES), F32)
    for c in reversed(range(n_chunks)):
        st_ref[c, :, LANES:2 * LANES] = state.astype(BF16)
        state = state * cd_b + kv_ref[c, LANES:2 * LANES, :]

    gain = gain_ref[...]

    def out_step(c, carry):
        sl = pl.ds(pl.multiple_of(c * c_len, c_len), c_len)
        qc = q_ref[sl, :]
        vc = v_ref[sl, :]
        ks = k_ref[sl, :].astype(F32) * (HEAD_DIM ** -0.5)
        q2 = jnp.concatenate([jnp.where(lane_lo, qc, jnp.zeros_like(qc)),
                              jnp.where(lane_lo, jnp.zeros_like(qc), qc)], axis=0)
        s = lax.dot_general(q2, ks.astype(BF16), NT_DIMS, preferred_element_type=F32)
        p = (s * decay2).astype(BF16)
        o2 = jnp.dot(p, vc, preferred_element_type=F32)
        inter = jnp.dot(qc, st_ref[c], preferred_element_type=F32)
        y = (jnp.where(lane_lo, o2[:c_len], o2[c_len:])
             + inter[:, 0:LANES] * xi_f + inter[:, LANES:2 * LANES] * xi_b)
        yn = _pair_head_norm(y, gain, lane_lo)
        o_ref[sl, :] = (yn * jax.nn.silu(g_ref[sl, :].astype(F32))).astype(o_ref.dtype)
        return carry

    lax.fori_loop(0, n_chunks, out_step, 0, unroll=4)


def _retention(proj_ret, decay_logit, gain, batch, seq):
    n_pairs = N_RET_HEADS // 2
    lg = jnp.repeat(decay_logit.astype(F32), HEAD_DIM, axis=-1)
    lg = lg.reshape(2, n_pairs, LANES).transpose(1, 0, 2)
    col = lambda off: (lambda b, p: (b, 0, off + p))
    blk = lambda off: pl.BlockSpec((None, seq, LANES), col(off))
    return pl.pallas_call(
        functools.partial(_ret_kernel, seq=seq),
        grid=(batch, n_pairs),
        in_specs=[
            pl.BlockSpec((1, 2, LANES), lambda b, p: (p, 0, 0)),
            pl.BlockSpec((1, LANES), lambda b, p: (0, p)),
            blk(0), blk(n_pairs), blk(2 * n_pairs), blk(3 * n_pairs),
        ],
        out_specs=pl.BlockSpec((None, seq, LANES), lambda b, p: (b, 0, p)),
        out_shape=jax.ShapeDtypeStruct((batch, seq, RET_WIDTH), BF16),
        scratch_shapes=[
            pltpu.VMEM((seq // RET_CHUNK, 2 * LANES, LANES), F32),
            pltpu.VMEM((seq // RET_CHUNK, LANES, 2 * LANES), BF16),
        ],
        compiler_params=_params(("parallel", "parallel")),
        name="retention",
    )(lg, gain.reshape(1, RET_WIDTH), proj_ret, proj_ret, proj_ret, proj_ret)


def _att_kernel(gain_ref, q_ref, k_ref, v_ref, o_ref,
                qd_ref, kd_ref, vd_ref, od_ref, ld_ref, op_ref, lp_ref, *, seq):
    qb_len = ATT_Q_BLOCK
    kb_len = ATT_K_BLOCK
    n_blocks = seq // qb_len
    pair = pl.program_id(1)
    lane = lax.broadcasted_iota(jnp.int32, (qb_len, LANES), 1)
    lane_lo = lane < HEAD_DIM

    srow = lax.broadcasted_iota(jnp.int32, (2 * qb_len, kb_len), 0)
    scol = lax.broadcasted_iota(jnp.int32, (2 * qb_len, kb_len), 1)
    head_in_pair = (srow >= qb_len).astype(F32)
    qrow = jnp.where(srow >= qb_len, srow - qb_len, srow)
    dist = jnp.abs(scol - HALF_SPAN - qrow)
    in_band = dist <= HALF_SPAN
    slope = jnp.exp(-math.log(2.0) * (2.0 * pair.astype(F32) + 1.0 + head_in_pair))
    base_bias = -slope * dist.astype(F32)
    kcol = lax.broadcasted_iota(jnp.int32, (1, kb_len), 1)

    zeros_pad = jnp.zeros((HALF_SPAN, LANES), F32)

    for pat, dil in enumerate(DILATIONS):
        sub_len = seq // dil
        cls_len = sub_len + 2 * HALF_SPAN
        blocks_per_cls = sub_len // qb_len
        bias = jnp.where(in_band, base_bias * float(dil), NEG_INF)

        for r in range(dil):
            kd_ref[r * cls_len:r * cls_len + HALF_SPAN, :] = zeros_pad
            vd_ref[r * cls_len:r * cls_len + HALF_SPAN, :] = zeros_pad
            kd_ref[r * cls_len + HALF_SPAN + sub_len:(r + 1) * cls_len, :] = zeros_pad
            vd_ref[r * cls_len + HALF_SPAN + sub_len:(r + 1) * cls_len, :] = zeros_pad
            for j in range(blocks_per_cls):
                if dil == 1:
                    src = pl.ds(j * qb_len, qb_len)
                else:
                    src = pl.ds(r + j * qb_len * dil, qb_len, stride=dil)
                dst_q = pl.ds(r * sub_len + j * qb_len, qb_len)
                dst_k = pl.ds(r * cls_len + HALF_SPAN + j * qb_len, qb_len)
                if dil > 1:
                    qd_ref[dst_q, :] = q_ref[src, :]
                kd_ref[dst_k, :] = k_ref[src, :]
                vd_ref[dst_k, :] = v_ref[src, :]

        q_src = q_ref if dil == 1 else qd_ref
        o_dst = op_ref.at[pat] if dil == 1 else od_ref
        l_dst = lp_ref.at[pat] if dil == 1 else ld_ref

        def block_step(i, carry, cls_len=cls_len, sub_len=sub_len,
                       blocks_per_cls=blocks_per_cls, bias=bias,
                       q_src=q_src, o_dst=o_dst, l_dst=l_dst):
            r = i // blocks_per_cls
            m = i - r * blocks_per_cls
            q_off = pl.multiple_of(i * qb_len, qb_len)
            k_off = pl.multiple_of(r * cls_len + m * qb_len, HALF_SPAN)
            qb = q_src[pl.ds(q_off, qb_len), :] * (HEAD_DIM ** -0.5)
            q2 = jnp.concatenate([jnp.where(lane_lo, qb, 0.0),
                                  jnp.where(lane_lo, 0.0, qb)], axis=0).astype(BF16)
            kw = kd_ref[pl.ds(k_off, kb_len), :].astype(BF16)
            vw = vd_ref[pl.ds(k_off, kb_len), :].astype(BF16)
            s = lax.dot_general(q2, kw, NT_DIMS, preferred_element_type=F32)
            kpos = kcol + (m * qb_len - HALF_SPAN)
            valid = (kpos >= 0) & (kpos < sub_len)
            s = jnp.where(valid, s + bias, NEG_INF)
            mx = jnp.max(s, axis=-1, keepdims=True)
            p = jnp.exp(s - mx)
            l = jnp.sum(p, axis=-1, keepdims=True)
            o2 = jnp.dot(p.astype(BF16), vw, preferred_element_type=F32) / l
            lse = mx + jnp.log(l)
            o_dst[pl.ds(q_off, qb_len), :] = jnp.where(lane_lo, o2[:qb_len], o2[qb_len:])
            l_dst[pl.ds(q_off, qb_len), :] = jnp.where(
                lane_lo, jnp.broadcast_to(lse[:qb_len], (qb_len, LANES)),
                jnp.broadcast_to(lse[qb_len:], (qb_len, LANES)))
            return carry

        lax.fori_loop(0, n_blocks, block_step, 0, unroll=8)

        if dil > 1:
            for r in range(dil):
                for j in range(blocks_per_cls):
                    src = pl.ds(r * sub_len + j * qb_len, qb_len)
                    dst = pl.ds(r + j * qb_len * dil, qb_len, stride=dil)
                    op_ref[pat, dst, :] = od_ref[src, :]
                    lp_ref[pat, dst, :] = ld_ref[src, :]

    gain = gain_ref[...]

    def mix_step(i, carry):
        sl = pl.ds(pl.multiple_of(i * qb_len, qb_len), qb_len)
        l0 = lp_ref[0, sl, :]
        l1 = lp_ref[1, sl, :]
        l2 = lp_ref[2, sl, :]
        mm = jnp.maximum(jnp.maximum(l0, l1), l2)
        w0 = jnp.exp(l0 - mm)
        w1 = jnp.exp(l1 - mm)
        w2 = jnp.exp(l2 - mm)
        y = (w0 * op_ref[0, sl, :] + w1 * op_ref[1, sl, :] + w2 * op_ref[2, sl, :]) / (w0 + w1 + w2)
        o_ref[sl, :] = _pair_head_norm(y, gain, lane_lo).astype(o_ref.dtype)
        return carry

    lax.fori_loop(0, n_blocks, mix_step, 0, unroll=4)


def _dilated_attention(proj_att, gain, batch, seq):
    n_pairs = N_ATT_HEADS // 2
    max_cls_rows = max(d * (seq // d + 2 * HALF_SPAN) for d in DILATIONS)
    blk = lambda off: pl.BlockSpec((None, seq, LANES), lambda b, p: (b, 0, off + p))
    return pl.pallas_call(
        functools.partial(_att_kernel, seq=seq),
        grid=(batch, n_pairs),
        in_specs=[
            pl.BlockSpec((1, LANES), lambda b, p: (0, p)),
            blk(0), blk(n_pairs), blk(2 * n_pairs),
        ],
        out_specs=pl.BlockSpec((None, seq, LANES), lambda b, p: (b, 0, p)),
        out_shape=jax.ShapeDtypeStruct((batch, seq, ATT_WIDTH), BF16),
        scratch_shapes=[
            pltpu.VMEM((seq, LANES), F32),
            pltpu.VMEM((max_cls_rows, LANES), F32),
            pltpu.VMEM((max_cls_rows, LANES), F32),
            pltpu.VMEM((seq, LANES), F32),
            pltpu.VMEM((seq, LANES), F32),
            pltpu.VMEM((len(DILATIONS), seq, LANES), F32),
            pltpu.VMEM((len(DILATIONS), seq, LANES), F32),
        ],
        compiler_params=_params(("parallel", "parallel")),
        name="dilated_attention",
    )(gain.reshape(1, ATT_WIDTH), proj_att, proj_att, proj_att)


def _out_proj_kernel(x_ref, a_ref, b_ref, wa_ref, wb_ref, o_ref):
    y = jnp.dot(a_ref[...], wa_ref[...], preferred_element_type=F32)
    y = y + jnp.dot(b_ref[...], wb_ref[...], preferred_element_type=F32)
    o_ref[...] = x_ref[...] + y


def _out_proj(x2, ret_out, att_out, w_a, w_b, tm=512):
    n = x2.shape[0]
    return pl.pallas_call(
        _out_proj_kernel,
        grid=(n // tm,),
        in_specs=[
            pl.BlockSpec((tm, D_MODEL), lambda i: (i, 0)),
            pl.BlockSpec((tm, RET_WIDTH), lambda i: (i, 0)),
            pl.BlockSpec((tm, ATT_WIDTH), lambda i: (i, 0)),
            pl.BlockSpec((RET_WIDTH, D_MODEL), lambda i: (0, 0)),
            pl.BlockSpec((ATT_WIDTH, D_MODEL), lambda i: (0, 0)),
        ],
        out_specs=pl.BlockSpec((tm, D_MODEL), lambda i: (i, 0)),
        out_shape=jax.ShapeDtypeStruct((n, D_MODEL), F32),
        compiler_params=_params(("parallel",)),
        name="out_proj",
    )(x2, ret_out, att_out, w_a, w_b)


def _mem_kv_kernel(m_ref, g_ref, w_ref, o_ref):
    h = _rms(m_ref[...], g_ref[...]).astype(BF16)
    o_ref[...] = jnp.dot(h, w_ref[...], preferred_element_type=F32).astype(o_ref.dtype)


def _mem_kv(mem2, gain, w_kv, tm=512):
    n = mem2.shape[0]
    return pl.pallas_call(
        _mem_kv_kernel,
        grid=(n // tm,),
        in_specs=[
            pl.BlockSpec((tm, D_MODEL), lambda i: (i, 0)),
            pl.BlockSpec((1, D_MODEL), lambda i: (0, 0)),
            pl.BlockSpec((D_MODEL, 2 * D_MODEL), lambda i: (0, 0)),
        ],
        out_specs=pl.BlockSpec((tm, 2 * D_MODEL), lambda i: (i, 0)),
        out_shape=jax.ShapeDtypeStruct((n, 2 * D_MODEL), BF16),
        compiler_params=_params(("parallel",)),
        name="mem_kv",
    )(mem2, gain.reshape(1, D_MODEL), w_kv)


def _mem_attn_kernel(x_ref, g_ref, wq_ref, kv_ref, wo_ref, o_ref):
    x = x_ref[...]
    h = _rms(x, g_ref[...]).astype(BF16)
    q = jnp.dot(h, wq_ref[...], preferred_element_type=F32) * (MEM_HEAD_DIM ** -0.5)
    q = q.astype(BF16)
    outs = []
    for hd in range(N_MEM_HEADS):
        lo = hd * MEM_HEAD_DIM
        kh = kv_ref[:, lo:lo + MEM_HEAD_DIM]
        vh = kv_ref[:, D_MODEL + lo:D_MODEL + lo + MEM_HEAD_DIM]
        s = lax.dot_general(q[:, lo:lo + MEM_HEAD_DIM], kh, NT_DIMS, preferred_element_type=F32)
        mx = jnp.max(s, axis=-1, keepdims=True)
        p = jnp.exp(s - mx)
        p = p / jnp.sum(p, axis=-1, keepdims=True)
        outs.append(jnp.dot(p.astype(BF16), vh, preferred_element_type=F32).astype(BF16))
    o = jnp.concatenate(outs, axis=-1)
    o_ref[...] = x + jnp.dot(o, wo_ref[...], preferred_element_type=F32)


def _mem_attn(x3, gain, w_q, kv3, w_o, tm=512):
    batch, seq, _ = x3.shape
    mem_len = kv3.shape[1]
    return pl.pallas_call(
        _mem_attn_kernel,
        grid=(batch, seq // tm),
        in_specs=[
            pl.BlockSpec((None, tm, D_MODEL), lambda b, i: (b, i, 0)),
            pl.BlockSpec((1, D_MODEL), lambda b, i: (0, 0)),
            pl.BlockSpec((D_MODEL, D_MODEL), lambda b, i: (0, 0)),
            pl.BlockSpec((None, mem_len, 2 * D_MODEL), lambda b, i: (b, 0, 0)),
            pl.BlockSpec((D_MODEL, D_MODEL), lambda b, i: (0, 0)),
        ],
        out_specs=pl.BlockSpec((None, tm, D_MODEL), lambda b, i: (b, i, 0)),
        out_shape=jax.ShapeDtypeStruct((batch, seq, D_MODEL), F32),
        compiler_params=_params(("parallel", "parallel")),
        name="mem_attn",
    )(x3, gain.reshape(1, D_MODEL), w_q, kv3, w_o)


def _bitonic_sort_desc(v):
    d = len(v) // 2
    while d >= 1:
        for k in range(len(v)):
            if k & d == 0:
                v[k], v[k + d] = jnp.maximum(v[k], v[k + d]), jnp.minimum(v[k], v[k + d])
        d //= 2


def _top16_pair_sums(t1, t2):
    top = [t1[0] + t2[q] for q in range(PEER_TOPK)]
    lists = [[t1[p] + t2[q] for q in range(PEER_TOPK // (p + 1))] for p in range(1, PEER_TOPK // 2)]
    lists.append([t1[p] + t2[0] for p in range(PEER_TOPK // 2, PEER_TOPK)])
    for other in lists:
        for k in range(PEER_TOPK - len(other), PEER_TOPK):
            top[k] = jnp.maximum(top[k], other[PEER_TOPK - 1 - k])
        _bitonic_sort_desc(top)
    return top


def _oddeven_merge_sort_pairs(n):
    pairs = []
    p = 1
    while p < n:
        k = p
        while k >= 1:
            for j in range(k % p, n - k, 2 * k):
                for i in range(min(k, n - j - k)):
                    if (i + j) // (p * 2) == (i + j + k) // (p * 2):
                        pairs.append((i + j, i + j + k))
            k //= 2
        p *= 2
    return pairs


def _sorted_top16(s_ref, hd, lanes):
    v = [s_ref[hd, 8 * i:8 * (i + 1), lanes] for i in range(PEER_TOPK)]

    def exchange(i, j):
        v[i], v[j] = jnp.maximum(v[i], v[j]), jnp.minimum(v[i], v[j])

    for i, j in _oddeven_merge_sort_pairs(PEER_TOPK):
        exchange(i, j)
    for shift in (4, 2, 1):
        v = [jnp.maximum(v[k], pltpu.roll(v[PEER_TOPK - 1 - k], shift, 0)) for k in range(PEER_TOPK)]
        _bitonic_sort_desc(v)
    return v


def _gelu_tanh(x):
    k0 = -2.0 * math.sqrt(2.0 / math.pi) * math.log2(math.e)
    return x / (1.0 + jnp.exp2(x * (k0 + (k0 * 0.044715) * (x * x))))


def _twin_bf16_words(v):
    bits = lax.bitcast_convert_type(v.astype(BF16).astype(F32), jnp.uint32)
    return bits | (bits >> 16)


def _peer_kernel(x_ref, g_ref, wqt_ref, kbd_ref, u_ref, vt_ref, o_ref,
                 ht_ref, qt_ref, s1_ref, s2_ref, r2_ref, b_ref, c_ref, a_ref, t1_ref, t2_ref,
                 cc_ref, invz_ref, acc_ref, *, tm, ec, sub):
    e = pl.program_id(1)
    n_chunks = pl.num_programs(1)
    slabs = ec // PEER_N_KEYS
    n_lane_blocks = tm // LANES

    zero = jnp.zeros((), BF16)

    def row_tile(words):
        return pltpu.bitcast(jnp.broadcast_to(words, (PEER_N_KEYS // 2, tm)), BF16)

    def gate_slab(il):
        grp = e * (slabs // 8) + il // 8
        wt = None
        for hd in range(PEER_HEADS):
            crow = row_tile(c_ref[hd, grp, il % 8:il % 8 + 1, :])
            arow = row_tile(a_ref[hd, grp, il % 8:il % 8 + 1, :])
            term = arow * jnp.where(r2_ref[hd] < crow, b_ref[hd], zero)
            wt = term if wt is None else wt + term
        return wt

    @pl.when(e == 0)
    def _select():
        ht_ref[...] = _rms(x_ref[...], g_ref[...]).T.astype(BF16)

        qt_ref[...] = jnp.dot(wqt_ref[...], ht_ref[...], preferred_element_type=F32).astype(BF16)

        def scores(hd, carry):
            rows = pl.ds(pl.multiple_of(hd * PEER_QUERY_DIM, PEER_QUERY_DIM), PEER_QUERY_DIM)
            s12 = jnp.dot(kbd_ref[...], qt_ref[rows, :], preferred_element_type=F32)
            s1_ref[hd] = s12[:PEER_N_KEYS]
            s2_ref[hd] = s12[PEER_N_KEYS:]
            return carry

        lax.fori_loop(0, PEER_HEADS, scores, 0)

        for hd in range(PEER_HEADS):
            def extract(lb, carry, hd=hd):
                lanes = pl.ds(pl.multiple_of(lb * LANES, LANES), LANES)
                top1 = _sorted_top16(s1_ref, hd, lanes)
                for it in range(PEER_TOPK):
                    t1_ref[it, hd:hd + 1, lanes] = top1[it][0:1]
                top2 = _sorted_top16(s2_ref, hd, lanes)
                for it in range(PEER_TOPK):
                    t2_ref[it, hd:hd + 1, lanes] = top2[it][0:1]
                top2 = [jnp.concatenate([t, t], axis=0) for t in top2]
                for grp in range(PEER_N_KEYS // 16):
                    rows = slice(grp * 16, (grp + 1) * 16)
                    s2 = s2_ref[hd, rows, lanes]
                    rank = jnp.full(s2.shape, float(PEER_N_KEYS), F32)
                    for it in reversed(range(PEER_TOPK)):
                        rank = jnp.where(s2 >= top2[it], float(it), rank)
                    r2_ref[hd, rows, lanes] = rank.astype(BF16)
                    b_ref[hd, rows, lanes] = jnp.exp(s2 - top2[0]).astype(BF16)
                return carry

            lax.fori_loop(0, n_lane_blocks, extract, 0)

        def compact(lb, carry):
            lanes = pl.ds(pl.multiple_of(lb * LANES, LANES), LANES)
            t1 = [t1_ref[p, :, lanes] for p in range(PEER_TOPK)]
            t2 = [t2_ref[q, :, lanes] for q in range(PEER_TOPK)]
            best = _top16_pair_sums(t1, t2)
            tau = best[PEER_TOPK - 1]
            z = jnp.ones_like(tau)
            for it in range(1, PEER_TOPK):
                z = z + jnp.exp(best[it] - best[0])
            invz_ref[:, lanes] = 1.0 / z
            for p in range(PEER_TOPK):
                cc = jnp.zeros_like(tau)
                for q in range(PEER_TOPK // (p + 1)):
                    cc = cc + jnp.where(t1[p] + t2[q] >= tau, 1.0, 0.0)
                cc_ref[p, :, lanes] = cc
            return carry

        lax.fori_loop(0, n_lane_blocks, compact, 0)

        for hd in range(PEER_HEADS):
            def rows(lb, carry, hd=hd):
                lanes = pl.ds(pl.multiple_of(lb * LANES, LANES), LANES)
                s1 = s1_ref[hd, :, lanes]
                cnt = jnp.zeros_like(s1)
                for p in reversed(range(PEER_TOPK)):
                    cnt = jnp.where(s1 >= t1_ref[p, hd:hd + 1, lanes], cc_ref[p, hd:hd + 1, lanes], cnt)
                a = jnp.exp(s1 - t1_ref[0, hd:hd + 1, lanes]) * invz_ref[hd:hd + 1, lanes]
                c_ref[hd, :, :, lanes] = _twin_bf16_words(cnt).reshape(PEER_N_KEYS // 8, 8, LANES)
                a_ref[hd, :, :, lanes] = _twin_bf16_words(a).reshape(PEER_N_KEYS // 8, 8, LANES)
                return carry

            lax.fori_loop(0, n_lane_blocks, rows, 0)
        acc_ref[...] = jnp.zeros_like(acc_ref)

    sub_slabs = sub // PEER_N_KEYS
    pieces = []
    for s in range(ec // sub):
        act = jnp.dot(u_ref[s * sub:(s + 1) * sub, :], ht_ref[...], preferred_element_type=F32)
        g = _gelu_tanh(act.astype(BF16))
        for k in range(sub_slabs):
            pieces.append(gate_slab(s * sub_slabs + k) * g[k * PEER_N_KEYS:(k + 1) * PEER_N_KEYS, :])
    p = jnp.concatenate(pieces, axis=0)
    acc_ref[...] += jnp.dot(vt_ref[...], p, preferred_element_type=F32)

    @pl.when(e == n_chunks - 1)
    def _finish():
        o_ref[...] = x_ref[...] + acc_ref[...].T


def _peer(x2, gain, wq_t, keys_bd, u, v_t, tm=512, ec=2048, sub=128):
    n = x2.shape[0]
    n_chunks = u.shape[0] // ec
    assert ec % (8 * PEER_N_KEYS) == 0 and ec % sub == 0 and sub % PEER_N_KEYS == 0
    kernel = functools.partial(_peer_kernel, tm=tm, ec=ec, sub=sub)
    head_rows = pltpu.VMEM((PEER_HEADS, PEER_N_KEYS, tm), F32)
    rank_rows = pltpu.VMEM((PEER_TOPK, PEER_HEADS, tm), F32)
    return pl.pallas_call(
        kernel,
        grid=(n // tm, n_chunks),
        in_specs=[
            pl.BlockSpec((tm, D_MODEL), lambda i, e: (i, 0)),
            pl.BlockSpec((1, D_MODEL), lambda i, e: (0, 0)),
            pl.BlockSpec((PEER_HEADS * PEER_QUERY_DIM, D_MODEL), lambda i, e: (0, 0)),
            pl.BlockSpec((2 * PEER_N_KEYS, PEER_QUERY_DIM), lambda i, e: (0, 0)),
            pl.BlockSpec((ec, D_MODEL), lambda i, e: (e, 0)),
            pl.BlockSpec((D_MODEL, ec), lambda i, e: (0, e)),
        ],
        out_specs=pl.BlockSpec((tm, D_MODEL), lambda i, e: (i, 0)),
        out_shape=jax.ShapeDtypeStruct((n, D_MODEL), F32),
        scratch_shapes=[
            pltpu.VMEM((D_MODEL, tm), BF16),
            pltpu.VMEM((PEER_HEADS * PEER_QUERY_DIM, tm), BF16),
            head_rows,
            head_rows,
            pltpu.VMEM((PEER_HEADS, PEER_N_KEYS, tm), BF16),
            pltpu.VMEM((PEER_HEADS, PEER_N_KEYS, tm), BF16),
            pltpu.VMEM((PEER_HEADS, PEER_N_KEYS // 8, 8, tm), jnp.uint32),
            pltpu.VMEM((PEER_HEADS, PEER_N_KEYS // 8, 8, tm), jnp.uint32),
            rank_rows,
            rank_rows,
            rank_rows,
            pltpu.VMEM((PEER_HEADS, tm), F32),
            pltpu.VMEM((D_MODEL, tm), F32),
        ],
        compiler_params=_params(("parallel", "arbitrary")),
        name="peer",
    )(x2, gain.reshape(1, D_MODEL), wq_t, keys_bd, u, v_t)


def _final_norm_kernel(x_ref, g_ref, o_ref):
    o_ref[...] = _rms(x_ref[...], g_ref[...])


def _final_norm(x2, gain, tm=1024):
    n = x2.shape[0]
    return pl.pallas_call(
        _final_norm_kernel,
        grid=(n // tm,),
        in_specs=[pl.BlockSpec((tm, D_MODEL), lambda i: (i, 0)),
                  pl.BlockSpec((1, D_MODEL), lambda i: (0, 0))],
        out_specs=pl.BlockSpec((tm, D_MODEL), lambda i: (i, 0)),
        out_shape=jax.ShapeDtypeStruct((n, D_MODEL), F32),
        compiler_params=_params(("parallel",)),
        name="final_norm",
    )(x2, gain.reshape(1, D_MODEL))


def kernel(x, mem, norm_mix, w_in, ret_decay_logit, ret_norm_gain, att_norm_gain, w_out,
           norm_mem, norm_mem_kv, w_mem_q, w_mem_kv, w_mem_o, norm_ffn,
           peer_w_query, peer_sub_keys, peer_expert_down, peer_expert_up, norm_final):
    batch, seq, _ = x.shape
    mem_len = mem.shape[1]
    depth = w_in.shape[0]
    n = batch * seq
    x2 = x.reshape(n, D_MODEL)
    mem2 = mem.reshape(batch * mem_len, D_MODEL)
    for layer in range(depth):
        w_in_l = w_in[layer].astype(BF16)
        proj_ret, proj_att = _in_proj(x2, norm_mix[layer], w_in_l[:, :RET_PROJ], w_in_l[:, RET_PROJ:])
        ret_out = _retention(proj_ret.reshape(batch, seq, RET_PROJ), ret_decay_logit[layer],
                             ret_norm_gain[layer], batch, seq)
        att_out = _dilated_attention(proj_att.reshape(batch, seq, ATT_PROJ), att_norm_gain[layer],
                                     batch, seq)
        w_out_l = w_out[layer].astype(BF16)
        x2 = _out_proj(x2, ret_out.reshape(n, RET_WIDTH), att_out.reshape(n, ATT_WIDTH),
                       w_out_l[:RET_WIDTH], w_out_l[RET_WIDTH:])
        kv = _mem_kv(mem2, norm_mem_kv[layer], w_mem_kv[layer].astype(BF16))
        x3 = _mem_attn(x2.reshape(batch, seq, D_MODEL), norm_mem[layer], w_mem_q[layer].astype(BF16),
                       kv.reshape(batch, mem_len, 2 * D_MODEL), w_mem_o[layer].astype(BF16))
        x2 = x3.reshape(n, D_MODEL)
        keys = peer_sub_keys[layer].astype(BF16)
        zeros = jnp.zeros_like(keys[0])
        keys_bd = jnp.concatenate([jnp.concatenate([keys[0], zeros], axis=1),
                                   jnp.concatenate([zeros, keys[1]], axis=1)], axis=0)
        x2 = _peer(x2, norm_ffn[layer],
                   peer_w_query[layer].T.astype(BF16), keys_bd,
                   peer_expert_down[layer].astype(BF16),
                   peer_expert_up[layer].T.astype(BF16))
    out = _final_norm(x2, norm_final)
    return out.reshape(batch, seq, D_MODEL)
```

```python
import functools
import math

import jax
import jax.numpy as jnp
from jax import lax
from jax.experimental import pallas as pl
from jax.experimental.pallas import tpu as pltpu

F32 = jnp.float32
BF16 = jnp.bfloat16

D_MODEL = 1024
HEAD_DIM = 64
N_RET_HEADS = 8
N_ATT_HEADS = 8
RET_WIDTH = N_RET_HEADS * HEAD_DIM
ATT_WIDTH = N_ATT_HEADS * HEAD_DIM
RET_PROJ = 4 * RET_WIDTH
ATT_PROJ = 3 * ATT_WIDTH
RET_CHUNK = 128
DILATIONS = (1, 4, 16)
HALF_SPAN = 64
N_MEM_HEADS = 4
MEM_HEAD_DIM = D_MODEL // N_MEM_HEADS
PEER_HEADS = 8
PEER_N_KEYS = 128
PEER_TOPK = 16
PEER_QUERY_DIM = 256
PEER_HALF = PEER_QUERY_DIM // 2
RMS_EPS = 1e-6
NEG_INF = -1e30
LANES = 128
ATT_Q_BLOCK = 128
ATT_K_BLOCK = ATT_Q_BLOCK + 2 * HALF_SPAN
VMEM_LIMIT = 56 * 1024 * 1024

NT_DIMS = (((1,), (1,)), ((), ()))


def _params(sem):
    return pltpu.CompilerParams(dimension_semantics=sem, vmem_limit_bytes=VMEM_LIMIT)


def _rms(x, gain):
    ms = jnp.mean(x * x, axis=-1, keepdims=True)
    return x * lax.rsqrt(ms + RMS_EPS) * gain


def _pair_head_norm(y, gain, lane_lo):
    y2 = y * y
    ms0 = jnp.sum(jnp.where(lane_lo, y2, 0.0), axis=-1, keepdims=True)
    ms1 = jnp.sum(jnp.where(lane_lo, 0.0, y2), axis=-1, keepdims=True)
    ms = jnp.where(lane_lo, ms0, ms1) * (1.0 / HEAD_DIM)
    return y * lax.rsqrt(ms + RMS_EPS) * gain


def _in_proj_kernel(x_ref, g_ref, wr_ref, wa_ref, or_ref, oa_ref):
    h = _rms(x_ref[...], g_ref[...]).astype(BF16)
    or_ref[...] = jnp.dot(h, wr_ref[...], preferred_element_type=F32).astype(or_ref.dtype)
    oa_ref[...] = jnp.dot(h, wa_ref[...], preferred_element_type=F32).astype(oa_ref.dtype)


def _in_proj(x2, gain, w_ret, w_att, tm=512):
    n = x2.shape[0]
    return pl.pallas_call(
        _in_proj_kernel,
        grid=(n // tm,),
        in_specs=[
            pl.BlockSpec((tm, D_MODEL), lambda i: (i, 0)),
            pl.BlockSpec((1, D_MODEL), lambda i: (0, 0)),
            pl.BlockSpec((D_MODEL, RET_PROJ), lambda i: (0, 0)),
            pl.BlockSpec((D_MODEL, ATT_PROJ), lambda i: (0, 0)),
        ],
        out_specs=[
            pl.BlockSpec((tm, RET_PROJ), lambda i: (i, 0)),
            pl.BlockSpec((tm, ATT_PROJ), lambda i: (i, 0)),
        ],
        out_shape=[
            jax.ShapeDtypeStruct((n, RET_PROJ), BF16),
            jax.ShapeDtypeStruct((n, ATT_PROJ), F32),
        ],
        compiler_params=_params(("parallel",)),
        name="in_proj",
    )(x2, gain.reshape(1, D_MODEL), w_ret, w_att)


def _ret_kernel(lg_ref, gain_ref, q_ref, k_ref, v_ref, g_ref, o_ref, kv_ref, st_ref, *, seq):
    c_len = RET_CHUNK
    n_chunks = seq // c_len
    lg = jax.nn.log_sigmoid(lg_ref[0])
    lgf = lg[0:1, :]
    lgb = lg[1:2, :]
    row = lax.broadcasted_iota(jnp.int32, (c_len, LANES), 0)
    lane = lax.broadcasted_iota(jnp.int32, (c_len, LANES), 1)
    lane_lo = lane < HEAD_DIM
    block_diag = (row < HEAD_DIM) == lane_lo
    rowf = row.astype(F32)
    rel = (row - lane).astype(F32)

    def decay_matrix(col):
        f = jnp.exp(lgf[:, col:col + 1] * jnp.maximum(rel, 0.0))
        b = jnp.exp(lgb[:, col:col + 1] * jnp.maximum(-rel, 0.0))
        return jnp.where(rel >= 0, f, b)

    decay2 = jnp.concatenate([decay_matrix(0), decay_matrix(HEAD_DIM)], axis=0)
    xi_f = jnp.exp(lgf * (rowf + 1.0))
    zeta_f = jnp.exp(lgf * (c_len - 1.0 - rowf))
    cd_f = jnp.exp(lgf * float(c_len))
    xi_b = jnp.exp(lgb * (c_len - rowf))
    zeta_b = jnp.exp(lgb * rowf)
    cd_b = jnp.exp(lgb * float(c_len))

    zeta2 = jnp.concatenate([zeta_f, zeta_b], axis=1)
    block_diag2 = jnp.concatenate([block_diag, block_diag], axis=0)

    def summary_step(c, carry):
        sl = pl.ds(pl.multiple_of(c * c_len, c_len), c_len)
        ks = k_ref[sl, :].astype(F32) * (HEAD_DIM ** -0.5)
        kz_t = (jnp.concatenate([ks, ks], axis=1) * zeta2).T.astype(BF16)
        upd = jnp.dot(kz_t, v_ref[sl, :], preferred_element_type=F32)
        kv_ref[c] = jnp.where(block_diag2, upd, 0.0)
        return carry

    lax.fori_loop(0, n_chunks, summary_step, 0, unroll=True)

    state = jnp.zeros((LANES, LANES), F32)
    for c in range(n_chunks):
        st_ref[c, :, 0:LANES] = state.astype(BF16)
        state = state * cd_f + kv_ref[c, 0:LANES, :]
    state = jnp.zeros((LANES, LANES), F32)
    for c in reversed(range(n_chunks)):
        st_ref[c, :, LANES:2 * LANES] = state.astype(BF16)
        state = state * cd_b + kv_ref[c, LANES:2 * LANES, :]

    gain = gain_ref[...]

    def out_step(c, carry):
        sl = pl.ds(pl.multiple_of(c * c_len, c_len), c_len)
        qc = q_ref[sl, :]
        vc = v_ref[sl, :]
        ks = k_ref[sl, :].astype(F32) * (HEAD_DIM ** -0.5)
        q2 = jnp.concatenate([jnp.where(lane_lo, qc, jnp.zeros_like(qc)),
                              jnp.where(lane_lo, jnp.zeros_like(qc), qc)], axis=0)
        s = lax.dot_general(q2, ks.astype(BF16), NT_DIMS, preferred_element_type=F32)
        p = (s * decay2).astype(BF16)
        o2 = jnp.dot(p, vc, preferred_element_type=F32)
        inter = jnp.dot(qc, st_ref[c], preferred_element_type=F32)
        y = (jnp.where(lane_lo, o2[:c_len], o2[c_len:])
             + inter[:, 0:LANES] * xi_f + inter[:, LANES:2 * LANES] * xi_b)
        yn = _pair_head_norm(y, gain, lane_lo)
        o_ref[sl, :] = (yn * jax.nn.silu(g_ref[sl, :].astype(F32))).astype(o_ref.dtype)
        return carry

    lax.fori_loop(0, n_chunks, out_step, 0, unroll=True)


def _retention(proj_ret, decay_logit, gain, batch, seq):
    n_pairs = N_RET_HEADS // 2
    lg = jnp.repeat(decay_logit.astype(F32), HEAD_DIM, axis=-1)
    lg = lg.reshape(2, n_pairs, LANES).transpose(1, 0, 2)
    col = lambda off: (lambda b, p: (b, 0, off + p))
    blk = lambda off: pl.BlockSpec((None, seq, LANES), col(off))
    return pl.pallas_call(
        functools.partial(_ret_kernel, seq=seq),
        grid=(batch, n_pairs),
        in_specs=[
            pl.BlockSpec((1, 2, LANES), lambda b, p: (p, 0, 0)),
            pl.BlockSpec((1, LANES), lambda b, p: (0, p)),
            blk(0), blk(n_pairs), blk(2 * n_pairs), blk(3 * n_pairs),
        ],
        out_specs=pl.BlockSpec((None, seq, LANES), lambda b, p: (b, 0, p)),
        out_shape=jax.ShapeDtypeStruct((batch, seq, RET_WIDTH), BF16),
        scratch_shapes=[
            pltpu.VMEM((seq // RET_CHUNK, 2 * LANES, LANES), F32),
            pltpu.VMEM((seq // RET_CHUNK, LANES, 2 * LANES), BF16),
        ],
        compiler_params=_params(("parallel", "parallel")),
        name="retention",
    )(lg, gain.reshape(1, RET_WIDTH), proj_ret, proj_ret, proj_ret, proj_ret)


def _att_kernel(gain_ref, q_ref, k_ref, v_ref, o_ref,
                qd_ref, kd_ref, vd_ref, od_ref, ld_ref, op_ref, lp_ref, *, seq):
    qb_len = ATT_Q_BLOCK
    kb_len = ATT_K_BLOCK
    n_blocks = seq // qb_len
    pair = pl.program_id(1)
    lane = lax.broadcasted_iota(jnp.int32, (qb_len, LANES), 1)
    lane_lo = lane < HEAD_DIM

    srow = lax.broadcasted_iota(jnp.int32, (2 * qb_len, kb_len), 0)
    scol = lax.broadcasted_iota(jnp.int32, (2 * qb_len, kb_len), 1)
    head_in_pair = (srow >= qb_len).astype(F32)
    qrow = jnp.where(srow >= qb_len, srow - qb_len, srow)
    dist = jnp.abs(scol - HALF_SPAN - qrow)
    in_band = dist <= HALF_SPAN
    slope = jnp.exp(-math.log(2.0) * (2.0 * pair.astype(F32) + 1.0 + head_in_pair))
    base_bias = -slope * dist.astype(F32)
    kcol = lax.broadcasted_iota(jnp.int32, (1, kb_len), 1)

    zeros_pad = jnp.zeros((HALF_SPAN, LANES), BF16)

    for pat, dil in enumerate(DILATIONS):
        sub_len = seq // dil
        cls_len = sub_len + 2 * HALF_SPAN
        blocks_per_cls = sub_len // qb_len
        bias = jnp.where(in_band, base_bias * float(dil), NEG_INF)

        for r in range(dil):
            kd_ref[r * cls_len:r * cls_len + HALF_SPAN, :] = zeros_pad
            vd_ref[r * cls_len:r * cls_len + HALF_SPAN, :] = zeros_pad
            kd_ref[r * cls_len + HALF_SPAN + sub_len:(r + 1) * cls_len, :] = zeros_pad
            vd_ref[r * cls_len + HALF_SPAN + sub_len:(r + 1) * cls_len, :] = zeros_pad
            for j in range(blocks_per_cls):
                if dil == 1:
                    src = pl.ds(j * qb_len, qb_len)
                else:
                    src = pl.ds(r + j * qb_len * dil, qb_len, stride=dil)
                dst_q = pl.ds(r * sub_len + j * qb_len, qb_len)
                dst_k = pl.ds(r * cls_len + HALF_SPAN + j * qb_len, qb_len)
                if dil > 1:
                    qd_ref[dst_q, :] = q_ref[src, :]
                kd_ref[dst_k, :] = k_ref[src, :].astype(BF16)
                vd_ref[dst_k, :] = v_ref[src, :].astype(BF16)

        q_src = q_ref if dil == 1 else qd_ref
        o_dst = op_ref.at[pat] if dil == 1 else od_ref
        l_dst = lp_ref.at[pat] if dil == 1 else ld_ref

        def block_step(i, carry, cls_len=cls_len, sub_len=sub_len,
                       blocks_per_cls=blocks_per_cls, bias=bias,
                       q_src=q_src, o_dst=o_dst, l_dst=l_dst):
            r = i // blocks_per_cls
            m = i - r * blocks_per_cls
            q_off = pl.multiple_of(i * qb_len, qb_len)
            k_off = pl.multiple_of(r * cls_len + m * qb_len, HALF_SPAN)
            qb = q_src[pl.ds(q_off, qb_len), :] * (HEAD_DIM ** -0.5)
            q2 = jnp.concatenate([jnp.where(lane_lo, qb, 0.0),
                                  jnp.where(lane_lo, 0.0, qb)], axis=0).astype(BF16)
            kw = kd_ref[pl.ds(k_off, kb_len), :]
            vw = vd_ref[pl.ds(k_off, kb_len), :]
            s = lax.dot_general(q2, kw, NT_DIMS, preferred_element_type=F32)
            kpos = kcol + (m * qb_len - HALF_SPAN)
            valid = (kpos >= 0) & (kpos < sub_len)
            s = jnp.where(valid, s + bias, NEG_INF)
            mx = jnp.max(s, axis=-1, keepdims=True)
            p = jnp.exp(s - mx)
            l = jnp.sum(p, axis=-1, keepdims=True)
            o2 = jnp.dot(p.astype(BF16), vw, preferred_element_type=F32) / l
            lse = mx + jnp.log(l)
            o_dst[pl.ds(q_off, qb_len), :] = jnp.where(lane_lo, o2[:qb_len], o2[qb_len:])
            l_dst[pl.ds(q_off, qb_len), :] = jnp.where(
                lane_lo, jnp.broadcast_to(lse[:qb_len], (qb_len, LANES)),
                jnp.broadcast_to(lse[qb_len:], (qb_len, LANES)))
            return carry

        lax.fori_loop(0, n_blocks, block_step, 0, unroll=True)

        if dil > 1:
            for r in range(dil):
                for j in range(blocks_per_cls):
                    src = pl.ds(r * sub_len + j * qb_len, qb_len)
                    dst = pl.ds(r + j * qb_len * dil, qb_len, stride=dil)
                    op_ref[pat, dst, :] = od_ref[src, :]
                    lp_ref[pat, dst, :] = ld_ref[src, :]

    gain = gain_ref[...]

    def mix_step(i, carry):
        sl = pl.ds(pl.multiple_of(i * qb_len, qb_len), qb_len)
        l0 = lp_ref[0, sl, :]
        l1 = lp_ref[1, sl, :]
        l2 = lp_ref[2, sl, :]
        mm = jnp.maximum(jnp.maximum(l0, l1), l2)
        w0 = jnp.exp(l0 - mm)
        w1 = jnp.exp(l1 - mm)
        w2 = jnp.exp(l2 - mm)
        y = (w0 * op_ref[0, sl, :] + w1 * op_ref[1, sl, :] + w2 * op_ref[2, sl, :]) / (w0 + w1 + w2)
        o_ref[sl, :] = _pair_head_norm(y, gain, lane_lo).astype(o_ref.dtype)
        return carry

    lax.fori_loop(0, n_blocks, mix_step, 0, unroll=True)


def _dilated_attention(proj_att, gain, batch, seq):
    n_pairs = N_ATT_HEADS // 2
    max_cls_rows = max(d * (seq // d + 2 * HALF_SPAN) for d in DILATIONS)
    blk = lambda off: pl.BlockSpec((None, seq, LANES), lambda b, p: (b, 0, off + p))
    return pl.pallas_call(
        functools.partial(_att_kernel, seq=seq),
        grid=(batch, n_pairs),
        in_specs=[
            pl.BlockSpec((1, LANES), lambda b, p: (0, p)),
            blk(0), blk(n_pairs), blk(2 * n_pairs),
        ],
        out_specs=pl.BlockSpec((None, seq, LANES), lambda b, p: (b, 0, p)),
        out_shape=jax.ShapeDtypeStruct((batch, seq, ATT_WIDTH), BF16),
        scratch_shapes=[
            pltpu.VMEM((seq, LANES), F32),
            pltpu.VMEM((max_cls_rows, LANES), BF16),
            pltpu.VMEM((max_cls_rows, LANES), BF16),
            pltpu.VMEM((seq, LANES), F32),
            pltpu.VMEM((seq, LANES), F32),
            pltpu.VMEM((len(DILATIONS), seq, LANES), F32),
            pltpu.VMEM((len(DILATIONS), seq, LANES), F32),
        ],
        compiler_params=_params(("parallel", "parallel")),
        name="dilated_attention",
    )(gain.reshape(1, ATT_WIDTH), proj_att, proj_att, proj_att)


def _out_proj_kernel(x_ref, a_ref, b_ref, wa_ref, wb_ref, o_ref):
    y = jnp.dot(a_ref[...], wa_ref[...], preferred_element_type=F32)
    y = y + jnp.dot(b_ref[...], wb_ref[...], preferred_element_type=F32)
    o_ref[...] = x_ref[...] + y


def _out_proj(x2, ret_out, att_out, w_a, w_b, tm=512):
    n = x2.shape[0]
    return pl.pallas_call(
        _out_proj_kernel,
        grid=(n // tm,),
        in_specs=[
            pl.BlockSpec((tm, D_MODEL), lambda i: (i, 0)),
            pl.BlockSpec((tm, RET_WIDTH), lambda i: (i, 0)),
            pl.BlockSpec((tm, ATT_WIDTH), lambda i: (i, 0)),
            pl.BlockSpec((RET_WIDTH, D_MODEL), lambda i: (0, 0)),
            pl.BlockSpec((ATT_WIDTH, D_MODEL), lambda i: (0, 0)),
        ],
        out_specs=pl.BlockSpec((tm, D_MODEL), lambda i: (i, 0)),
        out_shape=jax.ShapeDtypeStruct((n, D_MODEL), F32),
        compiler_params=_params(("parallel",)),
        name="out_proj",
    )(x2, ret_out, att_out, w_a, w_b)


def _mem_kv_kernel(m_ref, g_ref, w_ref, o_ref):
    h = _rms(m_ref[...], g_ref[...]).astype(BF16)
    o_ref[...] = jnp.dot(h, w_ref[...], preferred_element_type=F32).astype(o_ref.dtype)


def _mem_kv(mem2, gain, w_kv, tm=512):
    n = mem2.shape[0]
    return pl.pallas_call(
        _mem_kv_kernel,
        grid=(n // tm,),
        in_specs=[
            pl.BlockSpec((tm, D_MODEL), lambda i: (i, 0)),
            pl.BlockSpec((1, D_MODEL), lambda i: (0, 0)),
            pl.BlockSpec((D_MODEL, 2 * D_MODEL), lambda i: (0, 0)),
        ],
        out_specs=pl.BlockSpec((tm, 2 * D_MODEL), lambda i: (i, 0)),
        out_shape=jax.ShapeDtypeStruct((n, 2 * D_MODEL), BF16),
        compiler_params=_params(("parallel",)),
        name="mem_kv",
    )(mem2, gain.reshape(1, D_MODEL), w_kv)


def _mem_attn_kernel(x_ref, g_ref, wq_ref, kv_ref, wo_ref, o_ref):
    x = x_ref[...]
    h = _rms(x, g_ref[...]).astype(BF16)
    q = jnp.dot(h, wq_ref[...], preferred_element_type=F32) * (MEM_HEAD_DIM ** -0.5)
    q = q.astype(BF16)
    outs = []
    for hd in range(N_MEM_HEADS):
        lo = hd * MEM_HEAD_DIM
        kh = kv_ref[:, lo:lo + MEM_HEAD_DIM]
        vh = kv_ref[:, D_MODEL + lo:D_MODEL + lo + MEM_HEAD_DIM]
        s = lax.dot_general(q[:, lo:lo + MEM_HEAD_DIM], kh, NT_DIMS, preferred_element_type=F32)
        mx = jnp.max(s, axis=-1, keepdims=True)
        p = jnp.exp(s - mx)
        p = p / jnp.sum(p, axis=-1, keepdims=True)
        outs.append(jnp.dot(p.astype(BF16), vh, preferred_element_type=F32).astype(BF16))
    o = jnp.concatenate(outs, axis=-1)
    o_ref[...] = x + jnp.dot(o, wo_ref[...], preferred_element_type=F32)


def _mem_attn(x3, gain, w_q, kv3, w_o, tm=512):
    batch, seq, _ = x3.shape
    mem_len = kv3.shape[1]
    return pl.pallas_call(
        _mem_attn_kernel,
        grid=(batch, seq // tm),
        in_specs=[
            pl.BlockSpec((None, tm, D_MODEL), lambda b, i: (b, i, 0)),
            pl.BlockSpec((1, D_MODEL), lambda b, i: (0, 0)),
            pl.BlockSpec((D_MODEL, D_MODEL), lambda b, i: (0, 0)),
            pl.BlockSpec((None, mem_len, 2 * D_MODEL), lambda b, i: (b, 0, 0)),
            pl.BlockSpec((D_MODEL, D_MODEL), lambda b, i: (0, 0)),
        ],
        out_specs=pl.BlockSpec((None, tm, D_MODEL), lambda b, i: (b, i, 0)),
        out_shape=jax.ShapeDtypeStruct((batch, seq, D_MODEL), F32),
        compiler_params=_params(("parallel", "parallel")),
        name="mem_attn",
    )(x3, gain.reshape(1, D_MODEL), w_q, kv3, w_o)


def _bitonic_sort_desc(v):
    d = len(v) // 2
    while d >= 1:
        for k in range(len(v)):
            if k & d == 0:
                v[k], v[k + d] = jnp.maximum(v[k], v[k + d]), jnp.minimum(v[k], v[k + d])
        d //= 2


def _top16_pair_sums(t1, t2):
    top = [t1[0] + t2[q] for q in range(PEER_TOPK)]
    lists = [[t1[p] + t2[q] for q in range(PEER_TOPK // (p + 1))] for p in range(1, PEER_TOPK // 2)]
    lists.append([t1[p] + t2[0] for p in range(PEER_TOPK // 2, PEER_TOPK)])
    for other in lists:
        for k in range(PEER_TOPK - len(other), PEER_TOPK):
            top[k] = jnp.maximum(top[k], other[PEER_TOPK - 1 - k])
        _bitonic_sort_desc(top)
    return top


def _oddeven_merge_sort_pairs(n):
    pairs = []
    p = 1
    while p < n:
        k = p
        while k >= 1:
            for j in range(k % p, n - k, 2 * k):
                for i in range(min(k, n - j - k)):
                    if (i + j) // (p * 2) == (i + j + k) // (p * 2):
                        pairs.append((i + j, i + j + k))
            k //= 2
        p *= 2
    return pairs


def _sorted_top16(s_ref, hd, lanes):
    v = [s_ref[hd, 8 * i:8 * (i + 1), lanes] for i in range(PEER_TOPK)]

    def exchange(i, j):
        v[i], v[j] = jnp.maximum(v[i], v[j]), jnp.minimum(v[i], v[j])

    for i, j in _oddeven_merge_sort_pairs(PEER_TOPK):
        exchange(i, j)
    for shift in (4, 2, 1):
        v = [jnp.maximum(v[k], pltpu.roll(v[PEER_TOPK - 1 - k], shift, 0)) for k in range(PEER_TOPK)]
        _bitonic_sort_desc(v)
    return v


def _gelu_tanh(x):
    k0 = -2.0 * math.sqrt(2.0 / math.pi) * math.log2(math.e)
    return x / (1.0 + jnp.exp2(x * (k0 + (k0 * 0.044715) * (x * x))))


def _twin_bf16_words(v):
    bits = lax.bitcast_convert_type(v.astype(BF16).astype(F32), jnp.uint32)
    return bits | (bits >> 16)


def _peer_kernel(x_ref, g_ref, wqt_ref, kbd_ref, u_ref, vt_ref, o_ref,
                 ht_ref, qt_ref, s1_ref, s2_ref, r2_ref, b_ref, c_ref, a_ref, t1_ref, t2_ref,
                 cc_ref, invz_ref, acc_ref, *, tm, ec, sub):
    e = pl.program_id(1)
    n_chunks = pl.num_programs(1)
    slabs = ec // PEER_N_KEYS
    n_lane_blocks = tm // LANES

    zero = jnp.zeros((), BF16)

    def row_tile(words):
        return pltpu.bitcast(jnp.broadcast_to(words, (PEER_N_KEYS // 2, tm)), BF16)

    def gate_slab(il):
        grp = e * (slabs // 8) + il // 8
        wt = None
        for hd in range(PEER_HEADS):
            crow = row_tile(c_ref[hd, grp, il % 8:il % 8 + 1, :])
            arow = row_tile(a_ref[hd, grp, il % 8:il % 8 + 1, :])
            term = arow * jnp.where(r2_ref[hd] < crow, b_ref[hd], zero)
            wt = term if wt is None else wt + term
        return wt

    @pl.when(e == 0)
    def _select():
        ht_ref[...] = _rms(x_ref[...], g_ref[...]).T.astype(BF16)

        qt_ref[...] = jnp.dot(wqt_ref[...], ht_ref[...], preferred_element_type=F32).astype(BF16)

        def scores(hd, carry):
            rows = pl.ds(pl.multiple_of(hd * PEER_QUERY_DIM, PEER_QUERY_DIM), PEER_QUERY_DIM)
            s12 = jnp.dot(kbd_ref[...], qt_ref[rows, :], preferred_element_type=F32)
            s1_ref[hd] = s12[:PEER_N_KEYS]
            s2_ref[hd] = s12[PEER_N_KEYS:]
            return carry

        lax.fori_loop(0, PEER_HEADS, scores, 0, unroll=True)

        for hd in range(PEER_HEADS):
            def extract(lb, carry, hd=hd):
                lanes = pl.ds(pl.multiple_of(lb * LANES, LANES), LANES)
                top1 = _sorted_top16(s1_ref, hd, lanes)
                for it in range(PEER_TOPK):
                    t1_ref[it, hd:hd + 1, lanes] = top1[it][0:1]
                top2 = _sorted_top16(s2_ref, hd, lanes)
                for it in range(PEER_TOPK):
                    t2_ref[it, hd:hd + 1, lanes] = top2[it][0:1]
                top2 = [jnp.concatenate([t, t], axis=0) for t in top2]
                for grp in range(PEER_N_KEYS // 16):
                    rows = slice(grp * 16, (grp + 1) * 16)
                    s2 = s2_ref[hd, rows, lanes]
                    rank = jnp.full(s2.shape, float(PEER_N_KEYS), F32)
                    for it in reversed(range(PEER_TOPK)):
                        rank = jnp.where(s2 >= top2[it], float(it), rank)
                    r2_ref[hd, rows, lanes] = rank.astype(BF16)
                    b_ref[hd, rows, lanes] = jnp.exp(s2 - top2[0]).astype(BF16)
                return carry

            lax.fori_loop(0, n_lane_blocks, extract, 0)

        def compact(lb, carry):
            lanes = pl.ds(pl.multiple_of(lb * LANES, LANES), LANES)
            t1 = [t1_ref[p, :, lanes] for p in range(PEER_TOPK)]
            t2 = [t2_ref[q, :, lanes] for q in range(PEER_TOPK)]
            best = _top16_pair_sums(t1, t2)
            tau = best[PEER_TOPK - 1]
            z = jnp.ones_like(tau)
            for it in range(1, PEER_TOPK):
                z = z + jnp.exp(best[it] - best[0])
            invz_ref[:, lanes] = 1.0 / z
            for p in range(PEER_TOPK):
                cc = jnp.zeros_like(tau)
                for q in range(PEER_TOPK // (p + 1)):
                    cc = cc + jnp.where(t1[p] + t2[q] >= tau, 1.0, 0.0)
                cc_ref[p, :, lanes] = cc
            return carry

        lax.fori_loop(0, n_lane_blocks, compact, 0)

        for hd in range(PEER_HEADS):
            def rows(lb, carry, hd=hd):
                lanes = pl.ds(pl.multiple_of(lb * LANES, LANES), LANES)
                s1 = s1_ref[hd, :, lanes]
                cnt = jnp.zeros_like(s1)
                for p in reversed(range(PEER_TOPK)):
                    cnt = jnp.where(s1 >= t1_ref[p, hd:hd + 1, lanes], cc_ref[p, hd:hd + 1, lanes], cnt)
                a = jnp.exp(s1 - t1_ref[0, hd:hd + 1, lanes]) * invz_ref[hd:hd + 1, lanes]
                c_ref[hd, :, :, lanes] = _twin_bf16_words(cnt).reshape(PEER_N_KEYS // 8, 8, LANES)
                a_ref[hd, :, :, lanes] = _twin_bf16_words(a).reshape(PEER_N_KEYS // 8, 8, LANES)
                return carry

            lax.fori_loop(0, n_lane_blocks, rows, 0, unroll=True)
        acc_ref[...] = jnp.zeros_like(acc_ref)

    sub_slabs = sub // PEER_N_KEYS
    pieces = []
    for s in range(ec // sub):
        act = jnp.dot(u_ref[s * sub:(s + 1) * sub, :], ht_ref[...], preferred_element_type=F32)
        g = _gelu_tanh(act.astype(BF16))
        for k in range(sub_slabs):
            pieces.append(gate_slab(s * sub_slabs + k) * g[k * PEER_N_KEYS:(k + 1) * PEER_N_KEYS, :])
    p = jnp.concatenate(pieces, axis=0)
    acc_ref[...] += jnp.dot(vt_ref[...], p, preferred_element_type=F32)

    @pl.when(e == n_chunks - 1)
    def _finish():
        o_ref[...] = x_ref[...] + acc_ref[...].T


def _peer(x2, gain, wq_t, keys_bd, u, v_t, tm=512, ec=2048, sub=128):
    n = x2.shape[0]
    n_chunks = u.shape[0] // ec
    assert ec % (8 * PEER_N_KEYS) == 0 and ec % sub == 0 and sub % PEER_N_KEYS == 0
    kernel = functools.partial(_peer_kernel, tm=tm, ec=ec, sub=sub)
    head_rows = pltpu.VMEM((PEER_HEADS, PEER_N_KEYS, tm), F32)
    rank_rows = pltpu.VMEM((PEER_TOPK, PEER_HEADS, tm), F32)
    return pl.pallas_call(
        kernel,
        grid=(n // tm, n_chunks),
        in_specs=[
            pl.BlockSpec((tm, D_MODEL), lambda i, e: (i, 0)),
            pl.BlockSpec((1, D_MODEL), lambda i, e: (0, 0)),
            pl.BlockSpec((PEER_HEADS * PEER_QUERY_DIM, D_MODEL), lambda i, e: (0, 0)),
            pl.BlockSpec((2 * PEER_N_KEYS, PEER_QUERY_DIM), lambda i, e: (0, 0)),
            pl.BlockSpec((ec, D_MODEL), lambda i, e: (e, 0)),
            pl.BlockSpec((D_MODEL, ec), lambda i, e: (0, e)),
        ],
        out_specs=pl.BlockSpec((tm, D_MODEL), lambda i, e: (i, 0)),
        out_shape=jax.ShapeDtypeStruct((n, D_MODEL), F32),
        scratch_shapes=[
            pltpu.VMEM((D_MODEL, tm), BF16),
            pltpu.VMEM((PEER_HEADS * PEER_QUERY_DIM, tm), BF16),
            head_rows,
            head_rows,
            pltpu.VMEM((PEER_HEADS, PEER_N_KEYS, tm), BF16),
            pltpu.VMEM((PEER_HEADS, PEER_N_KEYS, tm), BF16),
            pltpu.VMEM((PEER_HEADS, PEER_N_KEYS // 8, 8, tm), jnp.uint32),
            pltpu.VMEM((PEER_HEADS, PEER_N_KEYS // 8, 8, tm), jnp.uint32),
            rank_rows,
            rank_rows,
            rank_rows,
            pltpu.VMEM((PEER_HEADS, tm), F32),
            pltpu.VMEM((D_MODEL, tm), F32),
        ],
        compiler_params=_params(("parallel", "arbitrary")),
        name="peer",
    )(x2, gain.reshape(1, D_MODEL), wq_t, keys_bd, u, v_t)


def _final_norm_kernel(x_ref, g_ref, o_ref):
    o_ref[...] = _rms(x_ref[...], g_ref[...])


def _final_norm(x2, gain, tm=1024):
    n = x2.shape[0]
    return pl.pallas_call(
        _final_norm_kernel,
        grid=(n // tm,),
        in_specs=[pl.BlockSpec((tm, D_MODEL), lambda i: (i, 0)),
                  pl.BlockSpec((1, D_MODEL), lambda i: (0, 0))],
        out_specs=pl.BlockSpec((tm, D_MODEL), lambda i: (i, 0)),
        out_shape=jax.ShapeDtypeStruct((n, D_MODEL), F32),
        compiler_params=_params(("parallel",)),
        name="final_norm",
    )(x2, gain.reshape(1, D_MODEL))


def kernel(x, mem, norm_mix, w_in, ret_decay_logit, ret_norm_gain, att_norm_gain, w_out,
           norm_mem, norm_mem_kv, w_mem_q, w_mem_kv, w_mem_o, norm_ffn,
           peer_w_query, peer_sub_keys, peer_expert_down, peer_expert_up, norm_final):
    batch, seq, _ = x.shape
    mem_len = mem.shape[1]
    depth = w_in.shape[0]
    n = batch * seq
    x2 = x.reshape(n, D_MODEL)
    mem2 = mem.reshape(batch * mem_len, D_MODEL)
    for layer in range(depth):
        w_in_l = w_in[layer].astype(BF16)
        proj_ret, proj_att = _in_proj(x2, norm_mix[layer], w_in_l[:, :RET_PROJ], w_in_l[:, RET_PROJ:])
        ret_out = _retention(proj_ret.reshape(batch, seq, RET_PROJ), ret_decay_logit[layer],
                             ret_norm_gain[layer], batch, seq)
        att_out = _dilated_attention(proj_att.reshape(batch, seq, ATT_PROJ), att_norm_gain[layer],
                                     batch, seq)
        w_out_l = w_out[layer].astype(BF16)
        x2 = _out_proj(x2, ret_out.reshape(n, RET_WIDTH), att_out.reshape(n, ATT_WIDTH),
                       w_out_l[:RET_WIDTH], w_out_l[RET_WIDTH:])
        kv = _mem_kv(mem2, norm_mem_kv[layer], w_mem_kv[layer].astype(BF16))
        x3 = _mem_attn(x2.reshape(batch, seq, D_MODEL), norm_mem[layer], w_mem_q[layer].astype(BF16),
                       kv.reshape(batch, mem_len, 2 * D_MODEL), w_mem_o[layer].astype(BF16))
        x2 = x3.reshape(n, D_MODEL)
        keys = peer_sub_keys[layer].astype(BF16)
        zeros = jnp.zeros_like(keys[0])
        keys_bd = jnp.concatenate([jnp.concatenate([keys[0], zeros], axis=1),
                                   jnp.concatenate([zeros, keys[1]], axis=1)], axis=0)
        x2 = _peer(x2, norm_ffn[layer],
                   peer_w_query[layer].T.astype(BF16), keys_bd,
                   peer_expert_down[layer].astype(BF16),
                   peer_expert_up[layer].T.astype(BF16))
    out = _final_norm(x2, norm_final)
    return out.reshape(batch, seq, D_MODEL)
```

```python
import functools
import math

import jax
import jax.numpy as jnp
from jax import lax
from jax.experimental import pallas as pl
from jax.experimental.pallas import tpu as pltpu

F32 = jnp.float32
BF16 = jnp.bfloat16

D_MODEL = 1024
HEAD_DIM = 64
N_RET_HEADS = 8
N_ATT_HEADS = 8
RET_WIDTH = N_RET_HEADS * HEAD_DIM
ATT_WIDTH = N_ATT_HEADS * HEAD_DIM
RET_PROJ = 4 * RET_WIDTH
ATT_PROJ = 3 * ATT_WIDTH
RET_CHUNK = 128
DILATIONS = (1, 4, 16)
HALF_SPAN = 64
N_MEM_HEADS = 4
MEM_HEAD_DIM = D_MODEL // N_MEM_HEADS
PEER_HEADS = 8
PEER_N_KEYS = 128
PEER_TOPK = 16
PEER_QUERY_DIM = 256
PEER_HALF = PEER_QUERY_DIM // 2
RMS_EPS = 1e-6
NEG_INF = -1e30
LANES = 128
ATT_Q_BLOCK = 128
ATT_K_BLOCK = ATT_Q_BLOCK + 2 * HALF_SPAN
VMEM_LIMIT = 56 * 1024 * 1024

NT_DIMS = (((1,), (1,)), ((), ()))


def _params(sem):
    return pltpu.CompilerParams(dimension_semantics=sem, vmem_limit_bytes=VMEM_LIMIT)


def _rms(x, gain):
    ms = jnp.mean(x * x, axis=-1, keepdims=True)
    return x * lax.rsqrt(ms + RMS_EPS) * gain


def _pair_head_norm(y, gain, lane_lo):
    y2 = y * y
    ms0 = jnp.sum(jnp.where(lane_lo, y2, 0.0), axis=-1, keepdims=True)
    ms1 = jnp.sum(jnp.where(lane_lo, 0.0, y2), axis=-1, keepdims=True)
    ms = jnp.where(lane_lo, ms0, ms1) * (1.0 / HEAD_DIM)
    return y * lax.rsqrt(ms + RMS_EPS) * gain


def _in_proj_kernel(x_ref, g_ref, wr_ref, wa_ref, or_ref, oa_ref):
    h = _rms(x_ref[...], g_ref[...]).astype(BF16)
    or_ref[...] = jnp.dot(h, wr_ref[...], preferred_element_type=F32).astype(or_ref.dtype)
    oa_ref[...] = jnp.dot(h, wa_ref[...], preferred_element_type=F32).astype(oa_ref.dtype)


def _in_proj(x2, gain, w_ret, w_att, tm=512):
    n = x2.shape[0]
    return pl.pallas_call(
        _in_proj_kernel,
        grid=(n // tm,),
        in_specs=[
            pl.BlockSpec((tm, D_MODEL), lambda i: (i, 0)),
            pl.BlockSpec((1, D_MODEL), lambda i: (0, 0)),
            pl.BlockSpec((D_MODEL, RET_PROJ), lambda i: (0, 0)),
            pl.BlockSpec((D_MODEL, ATT_PROJ), lambda i: (0, 0)),
        ],
        out_specs=[
            pl.BlockSpec((tm, RET_PROJ), lambda i: (i, 0)),
            pl.BlockSpec((tm, ATT_PROJ), lambda i: (i, 0)),
        ],
        out_shape=[
            jax.ShapeDtypeStruct((n, RET_PROJ), BF16),
            jax.ShapeDtypeStruct((n, ATT_PROJ), F32),
        ],
        compiler_params=_params(("parallel",)),
        name="in_proj",
    )(x2, gain.reshape(1, D_MODEL), w_ret, w_att)


def _ret_kernel(lg_ref, gain_ref, q_ref, k_ref, v_ref, g_ref, o_ref, kv_ref, st_ref, *, seq):
    c_len = RET_CHUNK
    n_chunks = seq // c_len
    lg = jax.nn.log_sigmoid(lg_ref[0])
    lgf = lg[0:1, :]
    lgb = lg[1:2, :]
    row = lax.broadcasted_iota(jnp.int32, (c_len, LANES), 0)
    lane = lax.broadcasted_iota(jnp.int32, (c_len, LANES), 1)
    lane_lo = lane < HEAD_DIM
    block_diag = (row < HEAD_DIM) == lane_lo
    rowf = row.astype(F32)
    rel = (row - lane).astype(F32)

    def decay_matrix(col):
        f = jnp.exp(lgf[:, col:col + 1] * jnp.maximum(rel, 0.0))
        b = jnp.exp(lgb[:, col:col + 1] * jnp.maximum(-rel, 0.0))
        return jnp.where(rel >= 0, f, b)

    decay2 = jnp.concatenate([decay_matrix(0), decay_matrix(HEAD_DIM)], axis=0)
    xi_f = jnp.exp(lgf * (rowf + 1.0))
    zeta_f = jnp.exp(lgf * (c_len - 1.0 - rowf))
    cd_f = jnp.exp(lgf * float(c_len))
    xi_b = jnp.exp(lgb * (c_len - rowf))
    zeta_b = jnp.exp(lgb * rowf)
    cd_b = jnp.exp(lgb * float(c_len))

    zeta2 = jnp.concatenate([zeta_f, zeta_b], axis=1)
    block_diag2 = jnp.concatenate([block_diag, block_diag], axis=0)

    def summary_step(c, carry):
        sl = pl.ds(pl.multiple_of(c * c_len, c_len), c_len)
        ks = k_ref[sl, :].astype(F32) * (HEAD_DIM ** -0.5)
        kz_t = (jnp.concatenate([ks, ks], axis=1) * zeta2).T.astype(BF16)
        upd = jnp.dot(kz_t, v_ref[sl, :], preferred_element_type=F32)
        kv_ref[c] = jnp.where(block_diag2, upd, 0.0)
        return carry

    lax.fori_loop(0, n_chunks, summary_step, 0, unroll=True)

    state = jnp.zeros((LANES, LANES), F32)
    for c in range(n_chunks):
        st_ref[c, :, 0:LANES] = state.astype(BF16)
        state = state * cd_f + kv_ref[c, 0:LANES, :]
    state = jnp.zeros((LANES, LANES), F32)
    for c in reversed(range(n_chunks)):
        st_ref[c, :, LANES:2 * LANES] = state.astype(BF16)
        state = state * cd_b + kv_ref[c, LANES:2 * LANES, :]

    gain = gain_ref[...]

    def out_step(c, carry):
        sl = pl.ds(pl.multiple_of(c * c_len, c_len), c_len)
        qc = q_ref[sl, :]
        vc = v_ref[sl, :]
        ks = k_ref[sl, :].astype(F32) * (HEAD_DIM ** -0.5)
        q2 = jnp.concatenate([jnp.where(lane_lo, qc, jnp.zeros_like(qc)),
                              jnp.where(lane_lo, jnp.zeros_like(qc), qc)], axis=0)
        s = lax.dot_general(q2, ks.astype(BF16), NT_DIMS, preferred_element_type=F32)
        p = (s * decay2).astype(BF16)
        o2 = jnp.dot(p, vc, preferred_element_type=F32)
        inter = jnp.dot(qc, st_ref[c], preferred_element_type=F32)
        y = (jnp.where(lane_lo, o2[:c_len], o2[c_len:])
             + inter[:, 0:LANES] * xi_f + inter[:, LANES:2 * LANES] * xi_b)
        yn = _pair_head_norm(y, gain, lane_lo)
        o_ref[sl, :] = (yn * jax.nn.silu(g_ref[sl, :].astype(F32))).astype(o_ref.dtype)
        return carry

    lax.fori_loop(0, n_chunks, out_step, 0, unroll=True)


def _retention(proj_ret, decay_logit, gain, batch, seq):
    n_pairs = N_RET_HEADS // 2
    lg = jnp.repeat(decay_logit.astype(F32), HEAD_DIM, axis=-1)
    lg = lg.reshape(2, n_pairs, LANES).transpose(1, 0, 2)
    col = lambda off: (lambda b, p: (b, 0, off + p))
    blk = lambda off: pl.BlockSpec((None, seq, LANES), col(off))
    return pl.pallas_call(
        functools.partial(_ret_kernel, seq=seq),
        grid=(batch, n_pairs),
        in_specs=[
            pl.BlockSpec((1, 2, LANES), lambda b, p: (p, 0, 0)),
            pl.BlockSpec((1, LANES), lambda b, p: (0, p)),
            blk(0), blk(n_pairs), blk(2 * n_pairs), blk(3 * n_pairs),
        ],
        out_specs=pl.BlockSpec((None, seq, LANES), lambda b, p: (b, 0, p)),
        out_shape=jax.ShapeDtypeStruct((batch, seq, RET_WIDTH), BF16),
        scratch_shapes=[
            pltpu.VMEM((seq // RET_CHUNK, 2 * LANES, LANES), F32),
            pltpu.VMEM((seq // RET_CHUNK, LANES, 2 * LANES), BF16),
        ],
        compiler_params=_params(("parallel", "parallel")),
        name="retention",
    )(lg, gain.reshape(1, RET_WIDTH), proj_ret, proj_ret, proj_ret, proj_ret)


def _att_kernel(gain_ref, q_ref, k_ref, v_ref, o_ref,
                qd_ref, kd_ref, vd_ref, od_ref, ld_ref, op_ref, lp_ref, *, seq):
    qb_len = ATT_Q_BLOCK
    kb_len = ATT_K_BLOCK
    n_blocks = seq // qb_len
    pair = pl.program_id(1)
    lane = lax.broadcasted_iota(jnp.int32, (qb_len, LANES), 1)
    lane_lo = lane < HEAD_DIM

    srow = lax.broadcasted_iota(jnp.int32, (2 * qb_len, kb_len), 0)
    scol = lax.broadcasted_iota(jnp.int32, (2 * qb_len, kb_len), 1)
    head_in_pair = (srow >= qb_len).astype(F32)
    qrow = jnp.where(srow >= qb_len, srow - qb_len, srow)
    dist = jnp.abs(scol - HALF_SPAN - qrow)
    in_band = dist <= HALF_SPAN
    slope = jnp.exp(-math.log(2.0) * (2.0 * pair.astype(F32) + 1.0 + head_in_pair))
    base_bias = -slope * dist.astype(F32)
    kcol = lax.broadcasted_iota(jnp.int32, (1, kb_len), 1)

    zeros_pad = jnp.zeros((HALF_SPAN, LANES), BF16)

    for pat, dil in enumerate(DILATIONS):
        sub_len = seq // dil
        cls_len = sub_len + 2 * HALF_SPAN
        blocks_per_cls = sub_len // qb_len
        bias = jnp.where(in_band, base_bias * float(dil), NEG_INF)

        for r in range(dil):
            kd_ref[r * cls_len:r * cls_len + HALF_SPAN, :] = zeros_pad
            vd_ref[r * cls_len:r * cls_len + HALF_SPAN, :] = zeros_pad
            kd_ref[r * cls_len + HALF_SPAN + sub_len:(r + 1) * cls_len, :] = zeros_pad
            vd_ref[r * cls_len + HALF_SPAN + sub_len:(r + 1) * cls_len, :] = zeros_pad
            for j in range(blocks_per_cls):
                if dil == 1:
                    src = pl.ds(j * qb_len, qb_len)
                else:
                    src = pl.ds(r + j * qb_len * dil, qb_len, stride=dil)
                dst_q = pl.ds(r * sub_len + j * qb_len, qb_len)
                dst_k = pl.ds(r * cls_len + HALF_SPAN + j * qb_len, qb_len)
                if dil > 1:
                    qd_ref[dst_q, :] = q_ref[src, :]
                kd_ref[dst_k, :] = k_ref[src, :].astype(BF16)
                vd_ref[dst_k, :] = v_ref[src, :].astype(BF16)

        q_src = q_ref if dil == 1 else qd_ref
        o_dst = op_ref.at[pat] if dil == 1 else od_ref
        l_dst = lp_ref.at[pat] if dil == 1 else ld_ref

        def block_step(i, carry, cls_len=cls_len, sub_len=sub_len,
                       blocks_per_cls=blocks_per_cls, bias=bias,
                       q_src=q_src, o_dst=o_dst, l_dst=l_dst):
            r = i // blocks_per_cls
            m = i - r * blocks_per_cls
            q_off = pl.multiple_of(i * qb_len, qb_len)
            k_off = pl.multiple_of(r * cls_len + m * qb_len, HALF_SPAN)
            qb = q_src[pl.ds(q_off, qb_len), :] * (HEAD_DIM ** -0.5)
            q2 = jnp.concatenate([jnp.where(lane_lo, qb, 0.0),
                                  jnp.where(lane_lo, 0.0, qb)], axis=0).astype(BF16)
            kw = kd_ref[pl.ds(k_off, kb_len), :]
            vw = vd_ref[pl.ds(k_off, kb_len), :]
            s = lax.dot_general(q2, kw, NT_DIMS, preferred_element_type=F32)
            kpos = kcol + (m * qb_len - HALF_SPAN)
            valid = (kpos >= 0) & (kpos < sub_len)
            s = jnp.where(valid, s + bias, NEG_INF)
            mx = jnp.max(s, axis=-1, keepdims=True)
            p = jnp.exp(s - mx)
            l = jnp.sum(p, axis=-1, keepdims=True)
            o2 = jnp.dot(p.astype(BF16), vw, preferred_element_type=F32) / l
            lse = mx + jnp.log(l)
            o_dst[pl.ds(q_off, qb_len), :] = jnp.where(lane_lo, o2[:qb_len], o2[qb_len:])
            l_dst[pl.ds(q_off, qb_len), :] = jnp.where(
                lane_lo, jnp.broadcast_to(lse[:qb_len], (qb_len, LANES)),
                jnp.broadcast_to(lse[qb_len:], (qb_len, LANES)))
            return carry

        lax.fori_loop(0, n_blocks, block_step, 0, unroll=True)

        if dil > 1:
            for r in range(dil):
                for j in range(blocks_per_cls):
                    src = pl.ds(r * sub_len + j * qb_len, qb_len)
                    dst = pl.ds(r + j * qb_len * dil, qb_len, stride=dil)
                    op_ref[pat, dst, :] = od_ref[src, :]
                    lp_ref[pat, dst, :] = ld_ref[src, :]

    gain = gain_ref[...]

    def mix_step(i, carry):
        sl = pl.ds(pl.multiple_of(i * qb_len, qb_len), qb_len)
        l0 = lp_ref[0, sl, :]
        l1 = lp_ref[1, sl, :]
        l2 = lp_ref[2, sl, :]
        mm = jnp.maximum(jnp.maximum(l0, l1), l2)
        w0 = jnp.exp(l0 - mm)
        w1 = jnp.exp(l1 - mm)
        w2 = jnp.exp(l2 - mm)
        y = (w0 * op_ref[0, sl, :] + w1 * op_ref[1, sl, :] + w2 * op_ref[2, sl, :]) / (w0 + w1 + w2)
        o_ref[sl, :] = _pair_head_norm(y, gain, lane_lo).astype(o_ref.dtype)
        return carry

    lax.fori_loop(0, n_blocks, mix_step, 0, unroll=True)


def _dilated_attention(proj_att, gain, batch, seq):
    n_pairs = N_ATT_HEADS // 2
    max_cls_rows = max(d * (seq // d + 2 * HALF_SPAN) for d in DILATIONS)
    blk = lambda off: pl.BlockSpec((None, seq, LANES), lambda b, p: (b, 0, off + p))
    return pl.pallas_call(
        functools.partial(_att_kernel, seq=seq),
        grid=(batch, n_pairs),
        in_specs=[
            pl.BlockSpec((1, LANES), lambda b, p: (0, p)),
            blk(0), blk(n_pairs), blk(2 * n_pairs),
        ],
        out_specs=pl.BlockSpec((None, seq, LANES), lambda b, p: (b, 0, p)),
        out_shape=jax.ShapeDtypeStruct((batch, seq, ATT_WIDTH), BF16),
        scratch_shapes=[
            pltpu.VMEM((seq, LANES), F32),
            pltpu.VMEM((max_cls_rows, LANES), BF16),
            pltpu.VMEM((max_cls_rows, LANES), BF16),
            pltpu.VMEM((seq, LANES), F32),
            pltpu.VMEM((seq, LANES), F32),
            pltpu.VMEM((len(DILATIONS), seq, LANES), F32),
            pltpu.VMEM((len(DILATIONS), seq, LANES), F32),
        ],
        compiler_params=_params(("parallel", "parallel")),
        name="dilated_attention",
    )(gain.reshape(1, ATT_WIDTH), proj_att, proj_att, proj_att)


def _mem_kv_kernel(m_ref, g_ref, w_ref, o_ref):
    h = _rms(m_ref[...], g_ref[...]).astype(BF16)
    o_ref[...] = jnp.dot(h, w_ref[...], preferred_element_type=F32).astype(o_ref.dtype)


def _mem_kv(mem2, gain, w_kv, tm=512):
    n = mem2.shape[0]
    return pl.pallas_call(
        _mem_kv_kernel,
        grid=(n // tm,),
        in_specs=[
            pl.BlockSpec((tm, D_MODEL), lambda i: (i, 0)),
            pl.BlockSpec((1, D_MODEL), lambda i: (0, 0)),
            pl.BlockSpec((D_MODEL, 2 * D_MODEL), lambda i: (0, 0)),
        ],
        out_specs=pl.BlockSpec((tm, 2 * D_MODEL), lambda i: (i, 0)),
        out_shape=jax.ShapeDtypeStruct((n, 2 * D_MODEL), BF16),
        compiler_params=_params(("parallel",)),
        name="mem_kv",
    )(mem2, gain.reshape(1, D_MODEL), w_kv)


def _mem_attn_kernel(x_ref, ra_ref, aa_ref, wa_ref, wb_ref, g_ref, wq_ref, kv_ref, wo_ref, o_ref):
    x = x_ref[...] + jnp.dot(ra_ref[...], wa_ref[...], preferred_element_type=F32)
    x = x + jnp.dot(aa_ref[...], wb_ref[...], preferred_element_type=F32)
    h = _rms(x, g_ref[...]).astype(BF16)
    q = jnp.dot(h, wq_ref[...], preferred_element_type=F32) * (MEM_HEAD_DIM ** -0.5)
    q = q.astype(BF16)
    outs = []
    for hd in range(N_MEM_HEADS):
        lo = hd * MEM_HEAD_DIM
        kh = kv_ref[:, lo:lo + MEM_HEAD_DIM]
        vh = kv_ref[:, D_MODEL + lo:D_MODEL + lo + MEM_HEAD_DIM]
        s = lax.dot_general(q[:, lo:lo + MEM_HEAD_DIM], kh, NT_DIMS, preferred_element_type=F32)
        mx = jnp.max(s, axis=-1, keepdims=True)
        p = jnp.exp(s - mx)
        p = p / jnp.sum(p, axis=-1, keepdims=True)
        outs.append(jnp.dot(p.astype(BF16), vh, preferred_element_type=F32).astype(BF16))
    o = jnp.concatenate(outs, axis=-1)
    o_ref[...] = x + jnp.dot(o, wo_ref[...], preferred_element_type=F32)


def _mem_attn(x3, ret_out, att_out, w_a, w_b, gain, w_q, kv3, w_o, tm=512):
    batch, seq, _ = x3.shape
    mem_len = kv3.shape[1]
    return pl.pallas_call(
        _mem_attn_kernel,
        grid=(batch, seq // tm),
        in_specs=[
            pl.BlockSpec((None, tm, D_MODEL), lambda b, i: (b, i, 0)),
            pl.BlockSpec((None, tm, RET_WIDTH), lambda b, i: (b, i, 0)),
            pl.BlockSpec((None, tm, ATT_WIDTH), lambda b, i: (b, i, 0)),
            pl.BlockSpec((RET_WIDTH, D_MODEL), lambda b, i: (0, 0)),
            pl.BlockSpec((ATT_WIDTH, D_MODEL), lambda b, i: (0, 0)),
            pl.BlockSpec((1, D_MODEL), lambda b, i: (0, 0)),
            pl.BlockSpec((D_MODEL, D_MODEL), lambda b, i: (0, 0)),
            pl.BlockSpec((None, mem_len, 2 * D_MODEL), lambda b, i: (b, 0, 0)),
            pl.BlockSpec((D_MODEL, D_MODEL), lambda b, i: (0, 0)),
        ],
        out_specs=pl.BlockSpec((None, tm, D_MODEL), lambda b, i: (b, i, 0)),
        out_shape=jax.ShapeDtypeStruct((batch, seq, D_MODEL), F32),
        compiler_params=_params(("parallel", "parallel")),
        name="mem_attn",
    )(x3, ret_out, att_out, w_a, w_b, gain.reshape(1, D_MODEL), w_q, kv3, w_o)


def _bitonic_sort_desc(v):
    d = len(v) // 2
    while d >= 1:
        for k in range(len(v)):
            if k & d == 0:
                v[k], v[k + d] = jnp.maximum(v[k], v[k + d]), jnp.minimum(v[k], v[k + d])
        d //= 2


def _top16_pair_sums(t1, t2):
    top = [t1[0] + t2[q] for q in range(PEER_TOPK)]
    lists = [[t1[p] + t2[q] for q in range(PEER_TOPK // (p + 1))] for p in range(1, PEER_TOPK // 2)]
    lists.append([t1[p] + t2[0] for p in range(PEER_TOPK // 2, PEER_TOPK)])
    for other in lists:
        for k in range(PEER_TOPK - len(other), PEER_TOPK):
            top[k] = jnp.maximum(top[k], other[PEER_TOPK - 1 - k])
        _bitonic_sort_desc(top)
    return top


def _oddeven_merge_sort_pairs(n):
    pairs = []
    p = 1
    while p < n:
        k = p
        while k >= 1:
            for j in range(k % p, n - k, 2 * k):
                for i in range(min(k, n - j - k)):
                    if (i + j) // (p * 2) == (i + j + k) // (p * 2):
                        pairs.append((i + j, i + j + k))
            k //= 2
        p *= 2
    return pairs


def _sorted_top16(s_ref, hd, lanes):
    v = [s_ref[hd, 8 * i:8 * (i + 1), lanes] for i in range(PEER_TOPK)]

    def exchange(i, j):
        v[i], v[j] = jnp.maximum(v[i], v[j]), jnp.minimum(v[i], v[j])

    for i, j in _oddeven_merge_sort_pairs(PEER_TOPK):
        exchange(i, j)
    for shift in (4, 2, 1):
        v = [jnp.maximum(v[k], pltpu.roll(v[PEER_TOPK - 1 - k], shift, 0)) for k in range(PEER_TOPK)]
        _bitonic_sort_desc(v)
    return v


def _gelu_tanh(x):
    k0 = -2.0 * math.sqrt(2.0 / math.pi) * math.log2(math.e)
    return x / (1.0 + jnp.exp2(x * (k0 + (k0 * 0.044715) * (x * x))))


def _twin_bf16_words(v):
    bits = lax.bitcast_convert_type(v.astype(BF16).astype(F32), jnp.uint32)
    return bits | (bits >> 16)


def _peer_kernel(x_ref, g_ref, wqt_ref, kbd_ref, u_ref, vt_ref, o_ref,
                 ht_ref, qt_ref, s1_ref, s2_ref, r2_ref, b_ref, c_ref, a_ref, t1_ref, t2_ref,
                 cc_ref, invz_ref, acc_ref, *, tm, ec, sub):
    e = pl.program_id(1)
    n_chunks = pl.num_programs(1)
    slabs = ec // PEER_N_KEYS
    n_lane_blocks = tm // LANES

    zero = jnp.zeros((), BF16)

    def row_tile(words):
        return pltpu.bitcast(jnp.broadcast_to(words, (PEER_N_KEYS // 2, tm)), BF16)

    def gate_slab(il):
        grp = e * (slabs // 8) + il // 8
        wt = None
        for hd in range(PEER_HEADS):
            crow = row_tile(c_ref[hd, grp, il % 8:il % 8 + 1, :])
            arow = row_tile(a_ref[hd, grp, il % 8:il % 8 + 1, :])
            term = arow * jnp.where(r2_ref[hd] < crow, b_ref[hd], zero)
            wt = term if wt is None else wt + term
        return wt

    @pl.when(e == 0)
    def _select():
        ht_ref[...] = _rms(x_ref[...], g_ref[...]).T.astype(BF16)

        qt_ref[...] = jnp.dot(wqt_ref[...], ht_ref[...], preferred_element_type=F32).astype(BF16)

        def scores(hd, carry):
            rows = pl.ds(pl.multiple_of(hd * PEER_QUERY_DIM, PEER_QUERY_DIM), PEER_QUERY_DIM)
            s12 = jnp.dot(kbd_ref[...], qt_ref[rows, :], preferred_element_type=F32)
            s1_ref[hd] = s12[:PEER_N_KEYS]
            s2_ref[hd] = s12[PEER_N_KEYS:]
            return carry

        lax.fori_loop(0, PEER_HEADS, scores, 0, unroll=True)

        for hd in range(PEER_HEADS):
            def extract(lb, carry, hd=hd):
                lanes = pl.ds(pl.multiple_of(lb * LANES, LANES), LANES)
                top1 = _sorted_top16(s1_ref, hd, lanes)
                for it in range(PEER_TOPK):
                    t1_ref[it, hd:hd + 1, lanes] = top1[it][0:1]
                top2 = _sorted_top16(s2_ref, hd, lanes)
                for it in range(PEER_TOPK):
                    t2_ref[it, hd:hd + 1, lanes] = top2[it][0:1]
                top2 = [jnp.concatenate([t, t], axis=0) for t in top2]
                for grp in range(PEER_N_KEYS // 16):
                    rows = slice(grp * 16, (grp + 1) * 16)
                    s2 = s2_ref[hd, rows, lanes]
                    rank = jnp.full(s2.shape, float(PEER_N_KEYS), F32)
                    for it in reversed(range(PEER_TOPK)):
                        rank = jnp.where(s2 >= top2[it], float(it), rank)
                    r2_ref[hd, rows, lanes] = rank.astype(BF16)
                    b_ref[hd, rows, lanes] = jnp.exp(s2 - top2[0]).astype(BF16)
                return carry

            lax.fori_loop(0, n_lane_blocks, extract, 0)

        def compact(lb, carry):
            lanes = pl.ds(pl.multiple_of(lb * LANES, LANES), LANES)
            t1 = [t1_ref[p, :, lanes] for p in range(PEER_TOPK)]
            t2 = [t2_ref[q, :, lanes] for q in range(PEER_TOPK)]
            best = _top16_pair_sums(t1, t2)
            tau = best[PEER_TOPK - 1]
            z = jnp.ones_like(tau)
            for it in range(1, PEER_TOPK):
                z = z + jnp.exp(best[it] - best[0])
            invz_ref[:, lanes] = 1.0 / z
            for p in range(PEER_TOPK):
                cc = jnp.zeros_like(tau)
                for q in range(PEER_TOPK // (p + 1)):
                    cc = cc + jnp.where(t1[p] + t2[q] >= tau, 1.0, 0.0)
                cc_ref[p, :, lanes] = cc
            return carry

        lax.fori_loop(0, n_lane_blocks, compact, 0)

        for hd in range(PEER_HEADS):
            def rows(lb, carry, hd=hd):
                lanes = pl.ds(pl.multiple_of(lb * LANES, LANES), LANES)
                s1 = s1_ref[hd, :, lanes]
                cnt = jnp.zeros_like(s1)
                for p in reversed(range(PEER_TOPK)):
                    cnt = jnp.where(s1 >= t1_ref[p, hd:hd + 1, lanes], cc_ref[p, hd:hd + 1, lanes], cnt)
                a = jnp.exp(s1 - t1_ref[0, hd:hd + 1, lanes]) * invz_ref[hd:hd + 1, lanes]
                c_ref[hd, :, :, lanes] = _twin_bf16_words(cnt).reshape(PEER_N_KEYS // 8, 8, LANES)
                a_ref[hd, :, :, lanes] = _twin_bf16_words(a).reshape(PEER_N_KEYS // 8, 8, LANES)
                return carry

            lax.fori_loop(0, n_lane_blocks, rows, 0, unroll=True)
        acc_ref[...] = jnp.zeros_like(acc_ref)

    sub_slabs = sub // PEER_N_KEYS
    pieces = []
    for s in range(ec // sub):
        act = jnp.dot(u_ref[s * sub:(s + 1) * sub, :], ht_ref[...], preferred_element_type=F32)
        g = _gelu_tanh(act.astype(BF16))
        for k in range(sub_slabs):
            pieces.append(gate_slab(s * sub_slabs + k) * g[k * PEER_N_KEYS:(k + 1) * PEER_N_KEYS, :])
    p = jnp.concatenate(pieces, axis=0)
    acc_ref[...] += jnp.dot(vt_ref[...], p, preferred_element_type=F32)

    @pl.when(e == n_chunks - 1)
    def _finish():
        o_ref[...] = x_ref[...] + acc_ref[...].T


def _peer(x2, gain, wq_t, keys_bd, u, v_t, tm=512, ec=2048, sub=128):
    n = x2.shape[0]
    n_chunks = u.shape[0] // ec
    assert ec % (8 * PEER_N_KEYS) == 0 and ec % sub == 0 and sub % PEER_N_KEYS == 0
    kernel = functools.partial(_peer_kernel, tm=tm, ec=ec, sub=sub)
    head_rows = pltpu.VMEM((PEER_HEADS, PEER_N_KEYS, tm), F32)
    rank_rows = pltpu.VMEM((PEER_TOPK, PEER_HEADS, tm), F32)
    return pl.pallas_call(
        kernel,
        grid=(n // tm, n_chunks),
        in_specs=[
            pl.BlockSpec((tm, D_MODEL), lambda i, e: (i, 0)),
            pl.BlockSpec((1, D_MODEL), lambda i, e: (0, 0)),
            pl.BlockSpec((PEER_HEADS * PEER_QUERY_DIM, D_MODEL), lambda i, e: (0, 0)),
            pl.BlockSpec((2 * PEER_N_KEYS, PEER_QUERY_DIM), lambda i, e: (0, 0)),
            pl.BlockSpec((ec, D_MODEL), lambda i, e: (e, 0)),
            pl.BlockSpec((D_MODEL, ec), lambda i, e: (0, e)),
        ],
        out_specs=pl.BlockSpec((tm, D_MODEL), lambda i, e: (i, 0)),
        out_shape=jax.ShapeDtypeStruct((n, D_MODEL), F32),
        scratch_shapes=[
            pltpu.VMEM((D_MODEL, tm), BF16),
            pltpu.VMEM((PEER_HEADS * PEER_QUERY_DIM, tm), BF16),
            head_rows,
            head_rows,
            pltpu.VMEM((PEER_HEADS, PEER_N_KEYS, tm), BF16),
            pltpu.VMEM((PEER_HEADS, PEER_N_KEYS, tm), BF16),
            pltpu.VMEM((PEER_HEADS, PEER_N_KEYS // 8, 8, tm), jnp.uint32),
            pltpu.VMEM((PEER_HEADS, PEER_N_KEYS // 8, 8, tm), jnp.uint32),
            rank_rows,
            rank_rows,
            rank_rows,
            pltpu.VMEM((PEER_HEADS, tm), F32),
            pltpu.VMEM((D_MODEL, tm), F32),
        ],
        compiler_params=_params(("parallel", "arbitrary")),
        name="peer",
    )(x2, gain.reshape(1, D_MODEL), wq_t, keys_bd, u, v_t)


def _final_norm_kernel(x_ref, g_ref, o_ref):
    o_ref[...] = _rms(x_ref[...], g_ref[...])


def _final_norm(x2, gain, tm=1024):
    n = x2.shape[0]
    return pl.pallas_call(
        _final_norm_kernel,
        grid=(n // tm,),
        in_specs=[pl.BlockSpec((tm, D_MODEL), lambda i: (i, 0)),
                  pl.BlockSpec((1, D_MODEL), lambda i: (0, 0))],
        out_specs=pl.BlockSpec((tm, D_MODEL), lambda i: (i, 0)),
        out_shape=jax.ShapeDtypeStruct((n, D_MODEL), F32),
        compiler_params=_params(("parallel",)),
        name="final_norm",
    )(x2, gain.reshape(1, D_MODEL))


def kernel(x, mem, norm_mix, w_in, ret_decay_logit, ret_norm_gain, att_norm_gain, w_out,
           norm_mem, norm_mem_kv, w_mem_q, w_mem_kv, w_mem_o, norm_ffn,
           peer_w_query, peer_sub_keys, peer_expert_down, peer_expert_up, norm_final):
    batch, seq, _ = x.shape
    mem_len = mem.shape[1]
    depth = w_in.shape[0]
    n = batch * seq
    x2 = x.reshape(n, D_MODEL)
    mem2 = mem.reshape(batch * mem_len, D_MODEL)
    for layer in range(depth):
        w_in_l = w_in[layer].astype(BF16)
        proj_ret, proj_att = _in_proj(x2, norm_mix[layer], w_in_l[:, :RET_PROJ], w_in_l[:, RET_PROJ:])
        ret_out = _retention(proj_ret.reshape(batch, seq, RET_PROJ), ret_decay_logit[layer],
                             ret_norm_gain[layer], batch, seq)
        att_out = _dilated_attention(proj_att.reshape(batch, seq, ATT_PROJ), att_norm_gain[layer],
                                     batch, seq)
        w_out_l = w_out[layer].astype(BF16)
        kv = _mem_kv(mem2, norm_mem_kv[layer], w_mem_kv[layer].astype(BF16))
        x3 = _mem_attn(x2.reshape(batch, seq, D_MODEL), ret_out, att_out,
                       w_out_l[:RET_WIDTH], w_out_l[RET_WIDTH:],
                       norm_mem[layer], w_mem_q[layer].astype(BF16),
                       kv.reshape(batch, mem_len, 2 * D_MODEL), w_mem_o[layer].astype(BF16))
        x2 = x3.reshape(n, D_MODEL)
        keys = peer_sub_keys[layer].astype(BF16)
        zeros = jnp.zeros_like(keys[0])
        keys_bd = jnp.concatenate([jnp.concatenate([keys[0], zeros], axis=1),
                                   jnp.concatenate([zeros, keys[1]], axis=1)], axis=0)
        x2 = _peer(x2, norm_ffn[layer],
                   peer_w_query[layer].T.astype(BF16), keys_bd,
                   peer_expert_down[layer].astype(BF16),
                   peer_expert_up[layer].T.astype(BF16))
    out = _final_norm(x2, norm_final)
    return out.reshape(batch, seq, D_MODEL)
```

```python
import functools
import math

import jax
import jax.numpy as jnp
from jax import lax
from jax.experimental import pallas as pl
from jax.experimental.pallas import tpu as pltpu

F32 = jnp.float32
BF16 = jnp.bfloat16

D_MODEL = 1024
HEAD_DIM = 64
N_RET_HEADS = 8
N_ATT_HEADS = 8
RET_WIDTH = N_RET_HEADS * HEAD_DIM
ATT_WIDTH = N_ATT_HEADS * HEAD_DIM
RET_PROJ = 4 * RET_WIDTH
ATT_PROJ = 3 * ATT_WIDTH
RET_CHUNK = 128
DILATIONS = (1, 4, 16)
HALF_SPAN = 64
N_MEM_HEADS = 4
MEM_HEAD_DIM = D_MODEL // N_MEM_HEADS
PEER_HEADS = 8
PEER_N_KEYS = 128
PEER_TOPK = 16
PEER_QUERY_DIM = 256
PEER_HALF = PEER_QUERY_DIM // 2
RMS_EPS = 1e-6
NEG_INF = -1e30
LANES = 128
ATT_Q_BLOCK = 128
ATT_K_BLOCK = ATT_Q_BLOCK + 2 * HALF_SPAN
VMEM_LIMIT = 56 * 1024 * 1024

NT_DIMS = (((1,), (1,)), ((), ()))


def _params(sem):
    return pltpu.CompilerParams(dimension_semantics=sem, vmem_limit_bytes=VMEM_LIMIT)


def _rms(x, gain):
    ms = jnp.mean(x * x, axis=-1, keepdims=True)
    return x * lax.rsqrt(ms + RMS_EPS) * gain


def _pair_head_norm(y, gain, lane_lo):
    y2 = y * y
    ms0 = jnp.sum(jnp.where(lane_lo, y2, 0.0), axis=-1, keepdims=True)
    ms1 = jnp.sum(jnp.where(lane_lo, 0.0, y2), axis=-1, keepdims=True)
    ms = jnp.where(lane_lo, ms0, ms1) * (1.0 / HEAD_DIM)
    return y * lax.rsqrt(ms + RMS_EPS) * gain


def _in_proj_kernel(x_ref, g_ref, wr_ref, wa_ref, or_ref, oa_ref):
    h = _rms(x_ref[...], g_ref[...]).astype(BF16)
    or_ref[...] = jnp.dot(h, wr_ref[...], preferred_element_type=F32).astype(or_ref.dtype)
    oa_ref[...] = jnp.dot(h, wa_ref[...], preferred_element_type=F32).astype(oa_ref.dtype)


def _in_proj(x2, gain, w_ret, w_att, tm=512):
    n = x2.shape[0]
    return pl.pallas_call(
        _in_proj_kernel,
        grid=(n // tm,),
        in_specs=[
            pl.BlockSpec((tm, D_MODEL), lambda i: (i, 0)),
            pl.BlockSpec((1, D_MODEL), lambda i: (0, 0)),
            pl.BlockSpec((D_MODEL, RET_PROJ), lambda i: (0, 0)),
            pl.BlockSpec((D_MODEL, ATT_PROJ), lambda i: (0, 0)),
        ],
        out_specs=[
            pl.BlockSpec((tm, RET_PROJ), lambda i: (i, 0)),
            pl.BlockSpec((tm, ATT_PROJ), lambda i: (i, 0)),
        ],
        out_shape=[
            jax.ShapeDtypeStruct((n, RET_PROJ), BF16),
            jax.ShapeDtypeStruct((n, ATT_PROJ), F32),
        ],
        compiler_params=_params(("parallel",)),
        name="in_proj",
    )(x2, gain.reshape(1, D_MODEL), w_ret, w_att)


def _ret_kernel(lg_ref, gain_ref, q_ref, k_ref, v_ref, g_ref, o_ref, kv_ref, st_ref, *, seq):
    c_len = RET_CHUNK
    n_chunks = seq // c_len
    lg = jax.nn.log_sigmoid(lg_ref[0])
    lgf = lg[0:1, :]
    lgb = lg[1:2, :]
    row = lax.broadcasted_iota(jnp.int32, (c_len, LANES), 0)
    lane = lax.broadcasted_iota(jnp.int32, (c_len, LANES), 1)
    lane_lo = lane < HEAD_DIM
    block_diag = (row < HEAD_DIM) == lane_lo
    rowf = row.astype(F32)
    rel = (row - lane).astype(F32)

    def decay_matrix(col):
        f = jnp.exp(lgf[:, col:col + 1] * jnp.maximum(rel, 0.0))
        b = jnp.exp(lgb[:, col:col + 1] * jnp.maximum(-rel, 0.0))
        return jnp.where(rel >= 0, f, b)

    decay2 = jnp.concatenate([decay_matrix(0), decay_matrix(HEAD_DIM)], axis=0)
    xi_f = jnp.exp(lgf * (rowf + 1.0))
    zeta_f = jnp.exp(lgf * (c_len - 1.0 - rowf))
    cd_f = jnp.exp(lgf * float(c_len))
    xi_b = jnp.exp(lgb * (c_len - rowf))
    zeta_b = jnp.exp(lgb * rowf)
    cd_b = jnp.exp(lgb * float(c_len))

    zeta2 = jnp.concatenate([zeta_f, zeta_b], axis=1)
    block_diag2 = jnp.concatenate([block_diag, block_diag], axis=0)

    def summary_step(c, carry):
        sl = pl.ds(pl.multiple_of(c * c_len, c_len), c_len)
        ks = k_ref[sl, :].astype(F32) * (HEAD_DIM ** -0.5)
        kz_t = (jnp.concatenate([ks, ks], axis=1) * zeta2).T.astype(BF16)
        upd = jnp.dot(kz_t, v_ref[sl, :], preferred_element_type=F32)
        kv_ref[c] = jnp.where(block_diag2, upd, 0.0)
        return carry

    lax.fori_loop(0, n_chunks, summary_step, 0, unroll=True)

    state = jnp.zeros((LANES, LANES), F32)
    for c in range(n_chunks):
        st_ref[c, :, 0:LANES] = state.astype(BF16)
        state = state * cd_f + kv_ref[c, 0:LANES, :]
    state = jnp.zeros((LANES, LANES), F32)
    for c in reversed(range(n_chunks)):
        st_ref[c, :, LANES:2 * LANES] = state.astype(BF16)
        state = state * cd_b + kv_ref[c, LANES:2 * LANES, :]

    gain = gain_ref[...]

    def out_step(c, carry):
        sl = pl.ds(pl.multiple_of(c * c_len, c_len), c_len)
        qc = q_ref[sl, :]
        vc = v_ref[sl, :]
        ks = k_ref[sl, :].astype(F32) * (HEAD_DIM ** -0.5)
        q2 = jnp.concatenate([jnp.where(lane_lo, qc, jnp.zeros_like(qc)),
                              jnp.where(lane_lo, jnp.zeros_like(qc), qc)], axis=0)
        s = lax.dot_general(q2, ks.astype(BF16), NT_DIMS, preferred_element_type=F32)
        p = (s * decay2).astype(BF16)
        o2 = jnp.dot(p, vc, preferred_element_type=F32)
        inter = jnp.dot(qc, st_ref[c], preferred_element_type=F32)
        y = (jnp.where(lane_lo, o2[:c_len], o2[c_len:])
             + inter[:, 0:LANES] * xi_f + inter[:, LANES:2 * LANES] * xi_b)
        yn = _pair_head_norm(y, gain, lane_lo)
        o_ref[sl, :] = (yn * jax.nn.silu(g_ref[sl, :].astype(F32))).astype(o_ref.dtype)
        return carry

    lax.fori_loop(0, n_chunks, out_step, 0, unroll=True)


def _retention(proj_ret, decay_logit, gain, batch, seq):
    n_pairs = N_RET_HEADS // 2
    lg = jnp.repeat(decay_logit.astype(F32), HEAD_DIM, axis=-1)
    lg = lg.reshape(2, n_pairs, LANES).transpose(1, 0, 2)
    col = lambda off: (lambda b, p: (b, 0, off + p))
    blk = lambda off: pl.BlockSpec((None, seq, LANES), col(off))
    return pl.pallas_call(
        functools.partial(_ret_kernel, seq=seq),
        grid=(batch, n_pairs),
        in_specs=[
            pl.BlockSpec((1, 2, LANES), lambda b, p: (p, 0, 0)),
            pl.BlockSpec((1, LANES), lambda b, p: (0, p)),
            blk(0), blk(n_pairs), blk(2 * n_pairs), blk(3 * n_pairs),
        ],
        out_specs=pl.BlockSpec((None, seq, LANES), lambda b, p: (b, 0, p)),
        out_shape=jax.ShapeDtypeStruct((batch, seq, RET_WIDTH), BF16),
        scratch_shapes=[
            pltpu.VMEM((seq // RET_CHUNK, 2 * LANES, LANES), F32),
            pltpu.VMEM((seq // RET_CHUNK, LANES, 2 * LANES), BF16),
        ],
        compiler_params=_params(("parallel", "parallel")),
        name="retention",
    )(lg, gain.reshape(1, RET_WIDTH), proj_ret, proj_ret, proj_ret, proj_ret)


def _att_kernel(gain_ref, q_ref, k_ref, v_ref, o_ref,
                qd_ref, kd_ref, vd_ref, od_ref, ld_ref, op_ref, lp_ref, *, seq):
    qb_len = ATT_Q_BLOCK
    kb_len = ATT_K_BLOCK
    n_blocks = seq // qb_len
    pair = pl.program_id(1)
    lane = lax.broadcasted_iota(jnp.int32, (qb_len, LANES), 1)
    lane_lo = lane < HEAD_DIM

    srow = lax.broadcasted_iota(jnp.int32, (2 * qb_len, kb_len), 0)
    scol = lax.broadcasted_iota(jnp.int32, (2 * qb_len, kb_len), 1)
    head_in_pair = (srow >= qb_len).astype(F32)
    qrow = jnp.where(srow >= qb_len, srow - qb_len, srow)
    dist = jnp.abs(scol - HALF_SPAN - qrow)
    in_band = dist <= HALF_SPAN
    slope = jnp.exp(-math.log(2.0) * (2.0 * pair.astype(F32) + 1.0 + head_in_pair))
    base_bias = -slope * dist.astype(F32)
    kcol = lax.broadcasted_iota(jnp.int32, (1, kb_len), 1)

    zeros_pad = jnp.zeros((HALF_SPAN, LANES), BF16)

    for pat, dil in enumerate(DILATIONS):
        sub_len = seq // dil
        cls_len = sub_len + 2 * HALF_SPAN
        blocks_per_cls = sub_len // qb_len
        bias = jnp.where(in_band, base_bias * float(dil), NEG_INF)

        for r in range(dil):
            kd_ref[r * cls_len:r * cls_len + HALF_SPAN, :] = zeros_pad
            vd_ref[r * cls_len:r * cls_len + HALF_SPAN, :] = zeros_pad
            kd_ref[r * cls_len + HALF_SPAN + sub_len:(r + 1) * cls_len, :] = zeros_pad
            vd_ref[r * cls_len + HALF_SPAN + sub_len:(r + 1) * cls_len, :] = zeros_pad
            for j in range(blocks_per_cls):
                if dil == 1:
                    src = pl.ds(j * qb_len, qb_len)
                else:
                    src = pl.ds(r + j * qb_len * dil, qb_len, stride=dil)
                dst_q = pl.ds(r * sub_len + j * qb_len, qb_len)
                dst_k = pl.ds(r * cls_len + HALF_SPAN + j * qb_len, qb_len)
                if dil > 1:
                    qd_ref[dst_q, :] = q_ref[src, :]
                kd_ref[dst_k, :] = k_ref[src, :].astype(BF16)
                vd_ref[dst_k, :] = v_ref[src, :].astype(BF16)

        q_src = q_ref if dil == 1 else qd_ref
        o_dst = op_ref.at[pat] if dil == 1 else od_ref
        l_dst = lp_ref.at[pat] if dil == 1 else ld_ref

        def block_step(i, carry, cls_len=cls_len, sub_len=sub_len,
                       blocks_per_cls=blocks_per_cls, bias=bias,
                       q_src=q_src, o_dst=o_dst, l_dst=l_dst):
            r = i // blocks_per_cls
            m = i - r * blocks_per_cls
            q_off = pl.multiple_of(i * qb_len, qb_len)
            k_off = pl.multiple_of(r * cls_len + m * qb_len, HALF_SPAN)
            qb = q_src[pl.ds(q_off, qb_len), :] * (HEAD_DIM ** -0.5)
            q2 = jnp.concatenate([jnp.where(lane_lo, qb, 0.0),
                                  jnp.where(lane_lo, 0.0, qb)], axis=0).astype(BF16)
            kw = kd_ref[pl.ds(k_off, kb_len), :]
            vw = vd_ref[pl.ds(k_off, kb_len), :]
            s = lax.dot_general(q2, kw, NT_DIMS, preferred_element_type=F32)
            kpos = kcol + (m * qb_len - HALF_SPAN)
            valid = (kpos >= 0) & (kpos < sub_len)
            s = jnp.where(valid, s + bias, NEG_INF)
            mx = jnp.max(s, axis=-1, keepdims=True)
            p = jnp.exp(s - mx)
            l = jnp.sum(p, axis=-1, keepdims=True)
            o2 = jnp.dot(p.astype(BF16), vw, preferred_element_type=F32) / l
            lse = mx + jnp.log(l)
            o_dst[pl.ds(q_off, qb_len), :] = jnp.where(lane_lo, o2[:qb_len], o2[qb_len:])
            l_dst[pl.ds(q_off, qb_len), :] = jnp.where(
                lane_lo, jnp.broadcast_to(lse[:qb_len], (qb_len, LANES)),
                jnp.broadcast_to(lse[qb_len:], (qb_len, LANES)))
            return carry

        lax.fori_loop(0, n_blocks, block_step, 0, unroll=True)

        if dil > 1:
            for r in range(dil):
                for j in range(blocks_per_cls):
                    src = pl.ds(r * sub_len + j * qb_len, qb_len)
                    dst = pl.ds(r + j * qb_len * dil, qb_len, stride=dil)
                    op_ref[pat, dst, :] = od_ref[src, :]
                    lp_ref[pat, dst, :] = ld_ref[src, :]

    gain = gain_ref[...]

    def mix_step(i, carry):
        sl = pl.ds(pl.multiple_of(i * qb_len, qb_len), qb_len)
        l0 = lp_ref[0, sl, :]
        l1 = lp_ref[1, sl, :]
        l2 = lp_ref[2, sl, :]
        mm = jnp.maximum(jnp.maximum(l0, l1), l2)
        w0 = jnp.exp(l0 - mm)
        w1 = jnp.exp(l1 - mm)
        w2 = jnp.exp(l2 - mm)
        y = (w0 * op_ref[0, sl, :] + w1 * op_ref[1, sl, :] + w2 * op_ref[2, sl, :]) / (w0 + w1 + w2)
        o_ref[sl, :] = _pair_head_norm(y, gain, lane_lo).astype(o_ref.dtype)
        return carry

    lax.fori_loop(0, n_blocks, mix_step, 0, unroll=True)


def _dilated_attention(proj_att, gain, batch, seq):
    n_pairs = N_ATT_HEADS // 2
    max_cls_rows = max(d * (seq // d + 2 * HALF_SPAN) for d in DILATIONS)
    blk = lambda off: pl.BlockSpec((None, seq, LANES), lambda b, p: (b, 0, off + p))
    return pl.pallas_call(
        functools.partial(_att_kernel, seq=seq),
        grid=(batch, n_pairs),
        in_specs=[
            pl.BlockSpec((1, LANES), lambda b, p: (0, p)),
            blk(0), blk(n_pairs), blk(2 * n_pairs),
        ],
        out_specs=pl.BlockSpec((None, seq, LANES), lambda b, p: (b, 0, p)),
        out_shape=jax.ShapeDtypeStruct((batch, seq, ATT_WIDTH), BF16),
        scratch_shapes=[
            pltpu.VMEM((seq, LANES), F32),
            pltpu.VMEM((max_cls_rows, LANES), BF16),
            pltpu.VMEM((max_cls_rows, LANES), BF16),
            pltpu.VMEM((seq, LANES), F32),
            pltpu.VMEM((seq, LANES), F32),
            pltpu.VMEM((len(DILATIONS), seq, LANES), F32),
            pltpu.VMEM((len(DILATIONS), seq, LANES), F32),
        ],
        compiler_params=_params(("parallel", "parallel")),
        name="dilated_attention",
    )(gain.reshape(1, ATT_WIDTH), proj_att, proj_att, proj_att)


def _mem_kv_kernel(m_ref, g_ref, w_ref, o_ref):
    h = _rms(m_ref[...], g_ref[...]).astype(BF16)
    o_ref[...] = jnp.dot(h, w_ref[...], preferred_element_type=F32).astype(o_ref.dtype)


def _mem_kv(mem2, gain, w_kv, tm=512):
    n = mem2.shape[0]
    return pl.pallas_call(
        _mem_kv_kernel,
        grid=(n // tm,),
        in_specs=[
            pl.BlockSpec((tm, D_MODEL), lambda i: (i, 0)),
            pl.BlockSpec((1, D_MODEL), lambda i: (0, 0)),
            pl.BlockSpec((D_MODEL, 2 * D_MODEL), lambda i: (0, 0)),
        ],
        out_specs=pl.BlockSpec((tm, 2 * D_MODEL), lambda i: (i, 0)),
        out_shape=jax.ShapeDtypeStruct((n, 2 * D_MODEL), BF16),
        compiler_params=_params(("parallel",)),
        name="mem_kv",
    )(mem2, gain.reshape(1, D_MODEL), w_kv)


def _mem_attn_kernel(x_ref, ra_ref, aa_ref, wa_ref, wb_ref, g_ref, wq_ref, kv_ref, wo_ref, o_ref):
    x = x_ref[...] + jnp.dot(ra_ref[...], wa_ref[...], preferred_element_type=F32)
    x = x + jnp.dot(aa_ref[...], wb_ref[...], preferred_element_type=F32)
    h = _rms(x, g_ref[...]).astype(BF16)
    q = jnp.dot(h, wq_ref[...], preferred_element_type=F32) * (MEM_HEAD_DIM ** -0.5)
    q = q.astype(BF16)
    outs = []
    for hd in range(N_MEM_HEADS):
        lo = hd * MEM_HEAD_DIM
        kh = kv_ref[:, lo:lo + MEM_HEAD_DIM]
        vh = kv_ref[:, D_MODEL + lo:D_MODEL + lo + MEM_HEAD_DIM]
        s = lax.dot_general(q[:, lo:lo + MEM_HEAD_DIM], kh, NT_DIMS, preferred_element_type=F32)
        mx = jnp.max(s, axis=-1, keepdims=True)
        p = jnp.exp(s - mx)
        p = p / jnp.sum(p, axis=-1, keepdims=True)
        outs.append(jnp.dot(p.astype(BF16), vh, preferred_element_type=F32).astype(BF16))
    o = jnp.concatenate(outs, axis=-1)
    o_ref[...] = x + jnp.dot(o, wo_ref[...], preferred_element_type=F32)


def _mem_attn(x3, ret_out, att_out, w_a, w_b, gain, w_q, kv3, w_o, tm=512):
    batch, seq, _ = x3.shape
    mem_len = kv3.shape[1]
    return pl.pallas_call(
        _mem_attn_kernel,
        grid=(batch, seq // tm),
        in_specs=[
            pl.BlockSpec((None, tm, D_MODEL), lambda b, i: (b, i, 0)),
            pl.BlockSpec((None, tm, RET_WIDTH), lambda b, i: (b, i, 0)),
            pl.BlockSpec((None, tm, ATT_WIDTH), lambda b, i: (b, i, 0)),
            pl.BlockSpec((RET_WIDTH, D_MODEL), lambda b, i: (0, 0)),
            pl.BlockSpec((ATT_WIDTH, D_MODEL), lambda b, i: (0, 0)),
            pl.BlockSpec((1, D_MODEL), lambda b, i: (0, 0)),
            pl.BlockSpec((D_MODEL, D_MODEL), lambda b, i: (0, 0)),
            pl.BlockSpec((None, mem_len, 2 * D_MODEL), lambda b, i: (b, 0, 0)),
            pl.BlockSpec((D_MODEL, D_MODEL), lambda b, i: (0, 0)),
        ],
        out_specs=pl.BlockSpec((None, tm, D_MODEL), lambda b, i: (b, i, 0)),
        out_shape=jax.ShapeDtypeStruct((batch, seq, D_MODEL), F32),
        compiler_params=_params(("parallel", "parallel")),
        name="mem_attn",
    )(x3, ret_out, att_out, w_a, w_b, gain.reshape(1, D_MODEL), w_q, kv3, w_o)


def _bitonic_sort_desc(v):
    d = len(v) // 2
    while d >= 1:
        for k in range(len(v)):
            if k & d == 0:
                v[k], v[k + d] = jnp.maximum(v[k], v[k + d]), jnp.minimum(v[k], v[k + d])
        d //= 2


def _top16_pair_sums(t1, t2):
    top = [t1[0] + t2[q] for q in range(PEER_TOPK)]
    lists = [[t1[p] + t2[q] for q in range(PEER_TOPK // (p + 1))] for p in range(1, PEER_TOPK // 2)]
    lists.append([t1[p] + t2[0] for p in range(PEER_TOPK // 2, PEER_TOPK)])
    for other in lists:
        for k in range(PEER_TOPK - len(other), PEER_TOPK):
            top[k] = jnp.maximum(top[k], other[PEER_TOPK - 1 - k])
        _bitonic_sort_desc(top)
    return top


def _oddeven_merge_sort_pairs(n):
    pairs = []
    p = 1
    while p < n:
        k = p
        while k >= 1:
            for j in range(k % p, n - k, 2 * k):
                for i in range(min(k, n - j - k)):
                    if (i + j) // (p * 2) == (i + j + k) // (p * 2):
                        pairs.append((i + j, i + j + k))
            k //= 2
        p *= 2
    return pairs


def _sorted_top16(s_ref, hd, lanes):
    v = [s_ref[hd, 8 * i:8 * (i + 1), lanes] for i in range(PEER_TOPK)]

    def exchange(i, j):
        v[i], v[j] = jnp.maximum(v[i], v[j]), jnp.minimum(v[i], v[j])

    for i, j in _oddeven_merge_sort_pairs(PEER_TOPK):
        exchange(i, j)
    for shift in (4, 2, 1):
        v = [jnp.maximum(v[k], pltpu.roll(v[PEER_TOPK - 1 - k], shift, 0)) for k in range(PEER_TOPK)]
        _bitonic_sort_desc(v)
    return v


def _gelu_tanh(x):
    k0 = -2.0 * math.sqrt(2.0 / math.pi) * math.log2(math.e)
    return x / (1.0 + jnp.exp2(x * (k0 + (k0 * 0.044715) * (x * x))))


def _twin_bf16_words(v):
    bits = lax.bitcast_convert_type(v.astype(BF16).astype(F32), jnp.uint32)
    return bits | (bits >> 16)


def _peer_kernel(x_ref, g_ref, og_ref, wqt_ref, kbd_ref, u_ref, vt_ref, o_ref,
                 ht_ref, qt_ref, s1_ref, s2_ref, r2_ref, b_ref, c_ref, a_ref, t1_ref, t2_ref,
                 cc_ref, invz_ref, acc_ref, *, tm, ec, sub, norm_output):
    e = pl.program_id(1)
    n_chunks = pl.num_programs(1)
    slabs = ec // PEER_N_KEYS
    n_lane_blocks = tm // LANES

    zero = jnp.zeros((), BF16)

    def row_tile(words):
        return pltpu.bitcast(jnp.broadcast_to(words, (PEER_N_KEYS // 2, tm)), BF16)

    def gate_slab(il):
        grp = e * (slabs // 8) + il // 8
        wt = None
        for hd in range(PEER_HEADS):
            crow = row_tile(c_ref[hd, grp, il % 8:il % 8 + 1, :])
            arow = row_tile(a_ref[hd, grp, il % 8:il % 8 + 1, :])
            term = arow * jnp.where(r2_ref[hd] < crow, b_ref[hd], zero)
            wt = term if wt is None else wt + term
        return wt

    @pl.when(e == 0)
    def _select():
        ht_ref[...] = _rms(x_ref[...], g_ref[...]).T.astype(BF16)

        qt_ref[...] = jnp.dot(wqt_ref[...], ht_ref[...], preferred_element_type=F32).astype(BF16)

        def scores(hd, carry):
            rows = pl.ds(pl.multiple_of(hd * PEER_QUERY_DIM, PEER_QUERY_DIM), PEER_QUERY_DIM)
            s12 = jnp.dot(kbd_ref[...], qt_ref[rows, :], preferred_element_type=F32)
            s1_ref[hd] = s12[:PEER_N_KEYS]
            s2_ref[hd] = s12[PEER_N_KEYS:]
            return carry

        lax.fori_loop(0, PEER_HEADS, scores, 0, unroll=True)

        for hd in range(PEER_HEADS):
            def extract(lb, carry, hd=hd):
                lanes = pl.ds(pl.multiple_of(lb * LANES, LANES), LANES)
                top1 = _sorted_top16(s1_ref, hd, lanes)
                for it in range(PEER_TOPK):
                    t1_ref[it, hd:hd + 1, lanes] = top1[it][0:1]
                top2 = _sorted_top16(s2_ref, hd, lanes)
                for it in range(PEER_TOPK):
                    t2_ref[it, hd:hd + 1, lanes] = top2[it][0:1]
                top2 = [jnp.concatenate([t, t], axis=0) for t in top2]
                for grp in range(PEER_N_KEYS // 16):
                    rows = slice(grp * 16, (grp + 1) * 16)
                    s2 = s2_ref[hd, rows, lanes]
                    m8 = s2 >= top2[7]
                    m4 = s2 >= jnp.where(m8, top2[3], top2[11])
                    m2 = s2 >= jnp.where(m8, jnp.where(m4, top2[1], top2[5]),
                                         jnp.where(m4, top2[9], top2[13]))
                    m1 = s2 >= jnp.where(
                        m8,
                        jnp.where(m4, jnp.where(m2, top2[0], top2[2]), jnp.where(m2, top2[4], top2[6])),
                        jnp.where(m4, jnp.where(m2, top2[8], top2[10]), jnp.where(m2, top2[12], top2[14])))
                    rank = (jnp.where(m8, 0.0, 8.0) + jnp.where(m4, 0.0, 4.0)
                            + jnp.where(m2, 0.0, 2.0) + jnp.where(m1, 0.0, 1.0))
                    rank = jnp.where(s2 >= top2[PEER_TOPK - 1], rank, float(PEER_N_KEYS))
                    r2_ref[hd, rows, lanes] = rank.astype(BF16)
                    b_ref[hd, rows, lanes] = jnp.exp(s2 - top2[0]).astype(BF16)
                return carry

            lax.fori_loop(0, n_lane_blocks, extract, 0)

        def compact(lb, carry):
            lanes = pl.ds(pl.multiple_of(lb * LANES, LANES), LANES)
            t1 = [t1_ref[p, :, lanes] for p in range(PEER_TOPK)]
            t2 = [t2_ref[q, :, lanes] for q in range(PEER_TOPK)]
            best = _top16_pair_sums(t1, t2)
            tau = best[PEER_TOPK - 1]
            z = jnp.ones_like(tau)
            for it in range(1, PEER_TOPK):
                z = z + jnp.exp(best[it] - best[0])
            invz_ref[:, lanes] = 1.0 / z
            for p in range(PEER_TOPK):
                cc = jnp.zeros_like(tau)
                for q in range(PEER_TOPK // (p + 1)):
                    cc = cc + jnp.where(t1[p] + t2[q] >= tau, 1.0, 0.0)
                cc_ref[p, :, lanes] = cc
            return carry

        lax.fori_loop(0, n_lane_blocks, compact, 0)

        for hd in range(PEER_HEADS):
            def rows(lb, carry, hd=hd):
                lanes = pl.ds(pl.multiple_of(lb * LANES, LANES), LANES)
                s1 = s1_ref[hd, :, lanes]
                cnt = jnp.zeros_like(s1)
                for p in reversed(range(PEER_TOPK)):
                    cnt = jnp.where(s1 >= t1_ref[p, hd:hd + 1, lanes], cc_ref[p, hd:hd + 1, lanes], cnt)
                a = jnp.exp(s1 - t1_ref[0, hd:hd + 1, lanes]) * invz_ref[hd:hd + 1, lanes]
                c_ref[hd, :, :, lanes] = _twin_bf16_words(cnt).reshape(PEER_N_KEYS // 8, 8, LANES)
                a_ref[hd, :, :, lanes] = _twin_bf16_words(a).reshape(PEER_N_KEYS // 8, 8, LANES)
                return carry

            lax.fori_loop(0, n_lane_blocks, rows, 0, unroll=True)
        acc_ref[...] = jnp.zeros_like(acc_ref)

    sub_slabs = sub // PEER_N_KEYS
    pieces = []
    for s in range(ec // sub):
        act = jnp.dot(u_ref[s * sub:(s + 1) * sub, :], ht_ref[...], preferred_element_type=F32)
        g = _gelu_tanh(act.astype(BF16))
        for k in range(sub_slabs):
            pieces.append(gate_slab(s * sub_slabs + k) * g[k * PEER_N_KEYS:(k + 1) * PEER_N_KEYS, :])
    p = jnp.concatenate(pieces, axis=0)
    acc_ref[...] += jnp.dot(vt_ref[...], p, preferred_element_type=F32)

    @pl.when(e == n_chunks - 1)
    def _finish():
        y = x_ref[...] + acc_ref[...].T
        o_ref[...] = _rms(y, og_ref[...]) if norm_output else y


def _peer(x2, gain, out_gain, wq_t, keys_bd, u, v_t, norm_output, tm=512, ec=2048, sub=128):
    n = x2.shape[0]
    n_chunks = u.shape[0] // ec
    assert ec % (8 * PEER_N_KEYS) == 0 and ec % sub == 0 and sub % PEER_N_KEYS == 0
    kernel = functools.partial(_peer_kernel, tm=tm, ec=ec, sub=sub, norm_output=norm_output)
    head_rows = pltpu.VMEM((PEER_HEADS, PEER_N_KEYS, tm), F32)
    rank_rows = pltpu.VMEM((PEER_TOPK, PEER_HEADS, tm), F32)
    return pl.pallas_call(
        kernel,
        grid=(n // tm, n_chunks),
        in_specs=[
            pl.BlockSpec((tm, D_MODEL), lambda i, e: (i, 0)),
            pl.BlockSpec((1, D_MODEL), lambda i, e: (0, 0)),
            pl.BlockSpec((1, D_MODEL), lambda i, e: (0, 0)),
            pl.BlockSpec((PEER_HEADS * PEER_QUERY_DIM, D_MODEL), lambda i, e: (0, 0)),
            pl.BlockSpec((2 * PEER_N_KEYS, PEER_QUERY_DIM), lambda i, e: (0, 0)),
            pl.BlockSpec((ec, D_MODEL), lambda i, e: (e, 0)),
            pl.BlockSpec((D_MODEL, ec), lambda i, e: (0, e)),
        ],
        out_specs=pl.BlockSpec((tm, D_MODEL), lambda i, e: (i, 0)),
        out_shape=jax.ShapeDtypeStruct((n, D_MODEL), F32),
        scratch_shapes=[
            pltpu.VMEM((D_MODEL, tm), BF16),
            pltpu.VMEM((PEER_HEADS * PEER_QUERY_DIM, tm), BF16),
            head_rows,
            head_rows,
            pltpu.VMEM((PEER_HEADS, PEER_N_KEYS, tm), BF16),
            pltpu.VMEM((PEER_HEADS, PEER_N_KEYS, tm), BF16),
            pltpu.VMEM((PEER_HEADS, PEER_N_KEYS // 8, 8, tm), jnp.uint32),
            pltpu.VMEM((PEER_HEADS, PEER_N_KEYS // 8, 8, tm), jnp.uint32),
            rank_rows,
            rank_rows,
            rank_rows,
            pltpu.VMEM((PEER_HEADS, tm), F32),
            pltpu.VMEM((D_MODEL, tm), F32),
        ],
        compiler_params=_params(("parallel", "arbitrary")),
        name="peer",
    )(x2, gain.reshape(1, D_MODEL), out_gain.reshape(1, D_MODEL), wq_t, keys_bd, u, v_t)


def kernel(x, mem, norm_mix, w_in, ret_decay_logit, ret_norm_gain, att_norm_gain, w_out,
           norm_mem, norm_mem_kv, w_mem_q, w_mem_kv, w_mem_o, norm_ffn,
           peer_w_query, peer_sub_keys, peer_expert_down, peer_expert_up, norm_final):
    batch, seq, _ = x.shape
    mem_len = mem.shape[1]
    depth = w_in.shape[0]
    assert depth >= 1
    n = batch * seq
    x2 = x.reshape(n, D_MODEL)
    mem2 = mem.reshape(batch * mem_len, D_MODEL)
    for layer in range(depth):
        w_in_l = w_in[layer].astype(BF16)
        proj_ret, proj_att = _in_proj(x2, norm_mix[layer], w_in_l[:, :RET_PROJ], w_in_l[:, RET_PROJ:])
        ret_out = _retention(proj_ret.reshape(batch, seq, RET_PROJ), ret_decay_logit[layer],
                             ret_norm_gain[layer], batch, seq)
        att_out = _dilated_attention(proj_att.reshape(batch, seq, ATT_PROJ), att_norm_gain[layer],
                                     batch, seq)
        w_out_l = w_out[layer].astype(BF16)
        kv = _mem_kv(mem2, norm_mem_kv[layer], w_mem_kv[layer].astype(BF16))
        x3 = _mem_attn(x2.reshape(batch, seq, D_MODEL), ret_out, att_out,
                       w_out_l[:RET_WIDTH], w_out_l[RET_WIDTH:],
                       norm_mem[layer], w_mem_q[layer].astype(BF16),
                       kv.reshape(batch, mem_len, 2 * D_MODEL), w_mem_o[layer].astype(BF16))
        x2 = x3.reshape(n, D_MODEL)
        keys = peer_sub_keys[layer].astype(BF16)
        zeros = jnp.zeros_like(keys[0])
        keys_bd = jnp.concatenate([jnp.concatenate([keys[0], zeros], axis=1),
                                   jnp.concatenate([zeros, keys[1]], axis=1)], axis=0)
        x2 = _peer(x2, norm_ffn[layer], norm_final,
                   peer_w_query[layer].T.astype(BF16), keys_bd,
                   peer_expert_down[layer].astype(BF16),
                   peer_expert_up[layer].T.astype(BF16),
                   norm_output=(layer == depth - 1))
    return x2.reshape(batch, seq, D_MODEL)
```

```python
import functools
import math

import jax
import jax.numpy as jnp
from jax import lax
from jax.experimental import pallas as pl
from jax.experimental.pallas import tpu as pltpu

F32 = jnp.float32
BF16 = jnp.bfloat16

D_MODEL = 1024
HEAD_DIM = 64
N_RET_HEADS = 8
N_ATT_HEADS = 8
RET_WIDTH = N_RET_HEADS * HEAD_DIM
ATT_WIDTH = N_ATT_HEADS * HEAD_DIM
RET_PROJ = 4 * RET_WIDTH
ATT_PROJ = 3 * ATT_WIDTH
RET_CHUNK = 128
DILATIONS = (1, 4, 16)
HALF_SPAN = 64
N_MEM_HEADS = 4
MEM_HEAD_DIM = D_MODEL // N_MEM_HEADS
PEER_HEADS = 8
PEER_N_KEYS = 128
PEER_TOPK = 16
PEER_QUERY_DIM = 256
PEER_HALF = PEER_QUERY_DIM // 2
RMS_EPS = 1e-6
NEG_INF = -1e30
LANES = 128
ATT_Q_BLOCK = 128
ATT_K_BLOCK = ATT_Q_BLOCK + 2 * HALF_SPAN
VMEM_LIMIT = 56 * 1024 * 1024

NT_DIMS = (((1,), (1,)), ((), ()))


def _params(sem):
    return pltpu.CompilerParams(dimension_semantics=sem, vmem_limit_bytes=VMEM_LIMIT)


def _rms(x, gain):
    ms = jnp.mean(x * x, axis=-1, keepdims=True)
    return x * lax.rsqrt(ms + RMS_EPS) * gain


def _pair_head_norm(y, gain, lane_lo):
    y2 = y * y
    ms0 = jnp.sum(jnp.where(lane_lo, y2, 0.0), axis=-1, keepdims=True)
    ms1 = jnp.sum(jnp.where(lane_lo, 0.0, y2), axis=-1, keepdims=True)
    ms = jnp.where(lane_lo, ms0, ms1) * (1.0 / HEAD_DIM)
    return y * lax.rsqrt(ms + RMS_EPS) * gain


def _in_proj_kernel(x_ref, g_ref, wr_ref, wa_ref, or_ref, oa_ref):
    h = _rms(x_ref[...], g_ref[...]).astype(BF16)
    or_ref[...] = jnp.dot(h, wr_ref[...], preferred_element_type=F32).astype(or_ref.dtype)
    oa_ref[...] = jnp.dot(h, wa_ref[...], preferred_element_type=F32).astype(oa_ref.dtype)


def _in_proj(x2, gain, w_ret, w_att, tm=1024):
    n = x2.shape[0]
    return pl.pallas_call(
        _in_proj_kernel,
        grid=(n // tm,),
        in_specs=[
            pl.BlockSpec((tm, D_MODEL), lambda i: (i, 0)),
            pl.BlockSpec((1, D_MODEL), lambda i: (0, 0)),
            pl.BlockSpec((D_MODEL, RET_PROJ), lambda i: (0, 0), pipeline_mode=pl.Buffered(1)),
            pl.BlockSpec((D_MODEL, ATT_PROJ), lambda i: (0, 0), pipeline_mode=pl.Buffered(1)),
        ],
        out_specs=[
            pl.BlockSpec((tm, RET_PROJ), lambda i: (i, 0)),
            pl.BlockSpec((tm, ATT_PROJ), lambda i: (i, 0)),
        ],
        out_shape=[
            jax.ShapeDtypeStruct((n, RET_PROJ), BF16),
            jax.ShapeDtypeStruct((n, ATT_PROJ), F32),
        ],
        compiler_params=_params(("parallel",)),
        name="in_proj",
    )(x2, gain.reshape(1, D_MODEL), w_ret, w_att)


def _ret_kernel(lg_ref, gain_ref, q_ref, k_ref, v_ref, g_ref, o_ref, kv_ref, st_ref, *, seq):
    c_len = RET_CHUNK
    n_chunks = seq // c_len
    lg = jax.nn.log_sigmoid(lg_ref[0])
    lgf = lg[0:1, :]
    lgb = lg[1:2, :]
    row = lax.broadcasted_iota(jnp.int32, (c_len, LANES), 0)
    lane = lax.broadcasted_iota(jnp.int32, (c_len, LANES), 1)
    lane_lo = lane < HEAD_DIM
    block_diag = (row < HEAD_DIM) == lane_lo
    rowf = row.astype(F32)
    rel = (row - lane).astype(F32)

    def decay_matrix(col):
        f = jnp.exp(lgf[:, col:col + 1] * jnp.maximum(rel, 0.0))
        b = jnp.exp(lgb[:, col:col + 1] * jnp.maximum(-rel, 0.0))
        return jnp.where(rel >= 0, f, b)

    decay2 = jnp.concatenate([decay_matrix(0), decay_matrix(HEAD_DIM)], axis=0)
    xi_f = jnp.exp(lgf * (rowf + 1.0))
    zeta_f = jnp.exp(lgf * (c_len - 1.0 - rowf))
    cd_f = jnp.exp(lgf * float(c_len))
    xi_b = jnp.exp(lgb * (c_len - rowf))
    zeta_b = jnp.exp(lgb * rowf)
    cd_b = jnp.exp(lgb * float(c_len))

    zeta2 = jnp.concatenate([zeta_f, zeta_b], axis=1)
    block_diag2 = jnp.concatenate([block_diag, block_diag], axis=0)

    def summary_step(c, carry):
        sl = pl.ds(pl.multiple_of(c * c_len, c_len), c_len)
        ks = k_ref[sl, :].astype(F32) * (HEAD_DIM ** -0.5)
        kz_t = (jnp.concatenate([ks, ks], axis=1) * zeta2).T.astype(BF16)
        upd = jnp.dot(kz_t, v_ref[sl, :], preferred_element_type=F32)
        kv_ref[c] = jnp.where(block_diag2, upd, 0.0)
        return carry

    lax.fori_loop(0, n_chunks, summary_step, 0, unroll=True)

    state = jnp.zeros((LANES, LANES), F32)
    for c in range(n_chunks):
        st_ref[c, :, 0:LANES] = state.astype(BF16)
        state = state * cd_f + kv_ref[c, 0:LANES, :]
    state = jnp.zeros((LANES, LANES), F32)
    for c in reversed(range(n_chunks)):
        st_ref[c, :, LANES:2 * LANES] = state.astype(BF16)
        state = state * cd_b + kv_ref[c, LANES:2 * LANES, :]

    gain = gain_ref[...]

    def out_step(c, carry):
        sl = pl.ds(pl.multiple_of(c * c_len, c_len), c_len)
        qc = q_ref[sl, :]
        vc = v_ref[sl, :]
        ks = k_ref[sl, :].astype(F32) * (HEAD_DIM ** -0.5)
        q2 = jnp.concatenate([jnp.where(lane_lo, qc, jnp.zeros_like(qc)),
                              jnp.where(lane_lo, jnp.zeros_like(qc), qc)], axis=0)
        s = lax.dot_general(q2, ks.astype(BF16), NT_DIMS, preferred_element_type=F32)
        p = (s * decay2).astype(BF16)
        o2 = jnp.dot(p, vc, preferred_element_type=F32)
        inter = jnp.dot(qc, st_ref[c], preferred_element_type=F32)
        y = (jnp.where(lane_lo, o2[:c_len], o2[c_len:])
             + inter[:, 0:LANES] * xi_f + inter[:, LANES:2 * LANES] * xi_b)
        yn = _pair_head_norm(y, gain, lane_lo)
        o_ref[sl, :] = (yn * jax.nn.silu(g_ref[sl, :].astype(F32))).astype(o_ref.dtype)
        return carry

    lax.fori_loop(0, n_chunks, out_step, 0, unroll=True)


def _retention(proj_ret, decay_logit, gain, batch, seq):
    n_pairs = N_RET_HEADS // 2
    lg = jnp.repeat(decay_logit.astype(F32), HEAD_DIM, axis=-1)
    lg = lg.reshape(2, n_pairs, LANES).transpose(1, 0, 2)
    col = lambda off: (lambda b, p: (b, 0, off + p))
    blk = lambda off: pl.BlockSpec((None, seq, LANES), col(off))
    return pl.pallas_call(
        functools.partial(_ret_kernel, seq=seq),
        grid=(batch, n_pairs),
        in_specs=[
            pl.BlockSpec((1, 2, LANES), lambda b, p: (p, 0, 0)),
            pl.BlockSpec((1, LANES), lambda b, p: (0, p)),
            blk(0), blk(n_pairs), blk(2 * n_pairs), blk(3 * n_pairs),
        ],
        out_specs=pl.BlockSpec((None, seq, LANES), lambda b, p: (b, 0, p)),
        out_shape=jax.ShapeDtypeStruct((batch, seq, RET_WIDTH), BF16),
        scratch_shapes=[
            pltpu.VMEM((seq // RET_CHUNK, 2 * LANES, LANES), F32),
            pltpu.VMEM((seq // RET_CHUNK, LANES, 2 * LANES), BF16),
        ],
        compiler_params=_params(("parallel", "parallel")),
        name="retention",
    )(lg, gain.reshape(1, RET_WIDTH), proj_ret, proj_ret, proj_ret, proj_ret)


def _att_kernel(gain_ref, q_ref, k_ref, v_ref, o_ref,
                qd_ref, kd_ref, vd_ref, od_ref, ld_ref, op_ref, lp_ref, *, seq):
    qb_len = ATT_Q_BLOCK
    kb_len = ATT_K_BLOCK
    n_blocks = seq // qb_len
    pair = pl.program_id(1)
    lane = lax.broadcasted_iota(jnp.int32, (qb_len, LANES), 1)
    lane_lo = lane < HEAD_DIM

    srow = lax.broadcasted_iota(jnp.int32, (2 * qb_len, kb_len), 0)
    scol = lax.broadcasted_iota(jnp.int32, (2 * qb_len, kb_len), 1)
    head_in_pair = (srow >= qb_len).astype(F32)
    qrow = jnp.where(srow >= qb_len, srow - qb_len, srow)
    dist = jnp.abs(scol - HALF_SPAN - qrow)
    in_band = dist <= HALF_SPAN
    slope = jnp.exp(-math.log(2.0) * (2.0 * pair.astype(F32) + 1.0 + head_in_pair))
    base_bias = -slope * dist.astype(F32)
    kcol = lax.broadcasted_iota(jnp.int32, (1, kb_len), 1)

    zeros_pad = jnp.zeros((HALF_SPAN, LANES), BF16)

    for pat, dil in enumerate(DILATIONS):
        sub_len = seq // dil
        cls_len = sub_len + 2 * HALF_SPAN
        blocks_per_cls = sub_len // qb_len
        bias = jnp.where(in_band, base_bias * float(dil), NEG_INF)

        for r in range(dil):
            kd_ref[r * cls_len:r * cls_len + HALF_SPAN, :] = zeros_pad
            vd_ref[r * cls_len:r * cls_len + HALF_SPAN, :] = zeros_pad
            kd_ref[r * cls_len + HALF_SPAN + sub_len:(r + 1) * cls_len, :] = zeros_pad
            vd_ref[r * cls_len + HALF_SPAN + sub_len:(r + 1) * cls_len, :] = zeros_pad
            for j in range(blocks_per_cls):
                if dil == 1:
                    src = pl.ds(j * qb_len, qb_len)
                else:
                    src = pl.ds(r + j * qb_len * dil, qb_len, stride=dil)
                dst_q = pl.ds(r * sub_len + j * qb_len, qb_len)
                dst_k = pl.ds(r * cls_len + HALF_SPAN + j * qb_len, qb_len)
                if dil > 1:
                    qd_ref[dst_q, :] = q_ref[src, :]
                kd_ref[dst_k, :] = k_ref[src, :].astype(BF16)
                vd_ref[dst_k, :] = v_ref[src, :].astype(BF16)

        q_src = q_ref if dil == 1 else qd_ref
        o_dst = op_ref.at[pat] if dil == 1 else od_ref
        l_dst = lp_ref.at[pat] if dil == 1 else ld_ref

        def block_step(i, carry, cls_len=cls_len, sub_len=sub_len,
                       blocks_per_cls=blocks_per_cls, bias=bias,
                       q_src=q_src, o_dst=o_dst, l_dst=l_dst):
            r = i // blocks_per_cls
            m = i - r * blocks_per_cls
            q_off = pl.multiple_of(i * qb_len, qb_len)
            k_off = pl.multiple_of(r * cls_len + m * qb_len, HALF_SPAN)
            qb = q_src[pl.ds(q_off, qb_len), :] * (HEAD_DIM ** -0.5)
            q2 = jnp.concatenate([jnp.where(lane_lo, qb, 0.0),
                                  jnp.where(lane_lo, 0.0, qb)], axis=0).astype(BF16)
            kw = kd_ref[pl.ds(k_off, kb_len), :]
            vw = vd_ref[pl.ds(k_off, kb_len), :]
            s = lax.dot_general(q2, kw, NT_DIMS, preferred_element_type=F32)
            kpos = kcol + (m * qb_len - HALF_SPAN)
            valid = (kpos >= 0) & (kpos < sub_len)
            s = jnp.where(valid, s + bias, NEG_INF)
            mx = jnp.max(s, axis=-1, keepdims=True)
            p = jnp.exp(s - mx)
            l = jnp.sum(p, axis=-1, keepdims=True)
            o2 = jnp.dot(p.astype(BF16), vw, preferred_element_type=F32) / l
            lse = mx + jnp.log(l)
            o_dst[pl.ds(q_off, qb_len), :] = jnp.where(lane_lo, o2[:qb_len], o2[qb_len:])
            l_dst[pl.ds(q_off, qb_len), :] = jnp.where(
                lane_lo, jnp.broadcast_to(lse[:qb_len], (qb_len, LANES)),
                jnp.broadcast_to(lse[qb_len:], (qb_len, LANES)))
            return carry

        lax.fori_loop(0, n_blocks, block_step, 0, unroll=True)

        if dil > 1:
            for r in range(dil):
                for j in range(blocks_per_cls):
                    src = pl.ds(r * sub_len + j * qb_len, qb_len)
                    dst = pl.ds(r + j * qb_len * dil, qb_len, stride=dil)
                    op_ref[pat, dst, :] = od_ref[src, :]
                    lp_ref[pat, dst, :] = ld_ref[src, :]

    gain = gain_ref[...]

    def mix_step(i, carry):
        sl = pl.ds(pl.multiple_of(i * qb_len, qb_len), qb_len)
        l0 = lp_ref[0, sl, :]
        l1 = lp_ref[1, sl, :]
        l2 = lp_ref[2, sl, :]
        mm = jnp.maximum(jnp.maximum(l0, l1), l2)
        w0 = jnp.exp(l0 - mm)
        w1 = jnp.exp(l1 - mm)
        w2 = jnp.exp(l2 - mm)
        y = (w0 * op_ref[0, sl, :] + w1 * op_ref[1, sl, :] + w2 * op_ref[2, sl, :]) / (w0 + w1 + w2)
        o_ref[sl, :] = _pair_head_norm(y, gain, lane_lo).astype(o_ref.dtype)
        return carry

    lax.fori_loop(0, n_blocks, mix_step, 0, unroll=True)


def _dilated_attention(proj_att, gain, batch, seq):
    n_pairs = N_ATT_HEADS // 2
    max_cls_rows = max(d * (seq // d + 2 * HALF_SPAN) for d in DILATIONS)
    blk = lambda off: pl.BlockSpec((None, seq, LANES), lambda b, p: (b, 0, off + p))
    return pl.pallas_call(
        functools.partial(_att_kernel, seq=seq),
        grid=(batch, n_pairs),
        in_specs=[
            pl.BlockSpec((1, LANES), lambda b, p: (0, p)),
            blk(0), blk(n_pairs), blk(2 * n_pairs),
        ],
        out_specs=pl.BlockSpec((None, seq, LANES), lambda b, p: (b, 0, p)),
        out_shape=jax.ShapeDtypeStruct((batch, seq, ATT_WIDTH), BF16),
        scratch_shapes=[
            pltpu.VMEM((seq, LANES), F32),
            pltpu.VMEM((max_cls_rows, LANES), BF16),
            pltpu.VMEM((max_cls_rows, LANES), BF16),
            pltpu.VMEM((seq, LANES), F32),
            pltpu.VMEM((seq, LANES), F32),
            pltpu.VMEM((len(DILATIONS), seq, LANES), F32),
            pltpu.VMEM((len(DILATIONS), seq, LANES), F32),
        ],
        compiler_params=_params(("parallel", "parallel")),
        name="dilated_attention",
    )(gain.reshape(1, ATT_WIDTH), proj_att, proj_att, proj_att)


def _mem_kv_kernel(m_ref, g_ref, w_ref, o_ref):
    h = _rms(m_ref[...], g_ref[...]).astype(BF16)
    o_ref[...] = jnp.dot(h, w_ref[...], preferred_element_type=F32).astype(o_ref.dtype)


def _mem_kv(mem2, gain, w_kv, tm=512):
    n = mem2.shape[0]
    return pl.pallas_call(
        _mem_kv_kernel,
        grid=(n // tm,),
        in_specs=[
            pl.BlockSpec((tm, D_MODEL), lambda i: (i, 0)),
            pl.BlockSpec((1, D_MODEL), lambda i: (0, 0)),
            pl.BlockSpec((D_MODEL, 2 * D_MODEL), lambda i: (0, 0)),
        ],
        out_specs=pl.BlockSpec((tm, 2 * D_MODEL), lambda i: (i, 0)),
        out_shape=jax.ShapeDtypeStruct((n, 2 * D_MODEL), BF16),
        compiler_params=_params(("parallel",)),
        name="mem_kv",
    )(mem2, gain.reshape(1, D_MODEL), w_kv)


def _mem_attn_kernel(x_ref, ra_ref, aa_ref, wa_ref, wb_ref, g_ref, wq_ref, kv_ref, wo_ref, o_ref):
    x = x_ref[...] + jnp.dot(ra_ref[...], wa_ref[...], preferred_element_type=F32)
    x = x + jnp.dot(aa_ref[...], wb_ref[...], preferred_element_type=F32)
    h = _rms(x, g_ref[...]).astype(BF16)
    q = jnp.dot(h, wq_ref[...], preferred_element_type=F32) * (MEM_HEAD_DIM ** -0.5)
    q = q.astype(BF16)
    outs = []
    for hd in range(N_MEM_HEADS):
        lo = hd * MEM_HEAD_DIM
        kh = kv_ref[:, lo:lo + MEM_HEAD_DIM]
        vh = kv_ref[:, D_MODEL + lo:D_MODEL + lo + MEM_HEAD_DIM]
        s = lax.dot_general(q[:, lo:lo + MEM_HEAD_DIM], kh, NT_DIMS, preferred_element_type=F32)
        mx = jnp.max(s, axis=-1, keepdims=True)
        p = jnp.exp(s - mx)
        p = p / jnp.sum(p, axis=-1, keepdims=True)
        outs.append(jnp.dot(p.astype(BF16), vh, preferred_element_type=F32).astype(BF16))
    o = jnp.concatenate(outs, axis=-1)
    o_ref[...] = x + jnp.dot(o, wo_ref[...], preferred_element_type=F32)


def _mem_attn(x3, ret_out, att_out, w_a, w_b, gain, w_q, kv3, w_o, tm=512):
    batch, seq, _ = x3.shape
    mem_len = kv3.shape[1]
    return pl.pallas_call(
        _mem_attn_kernel,
        grid=(batch, seq // tm),
        in_specs=[
            pl.BlockSpec((None, tm, D_MODEL), lambda b, i: (b, i, 0)),
            pl.BlockSpec((None, tm, RET_WIDTH), lambda b, i: (b, i, 0)),
            pl.BlockSpec((None, tm, ATT_WIDTH), lambda b, i: (b, i, 0)),
            pl.BlockSpec((RET_WIDTH, D_MODEL), lambda b, i: (0, 0)),
            pl.BlockSpec((ATT_WIDTH, D_MODEL), lambda b, i: (0, 0)),
            pl.BlockSpec((1, D_MODEL), lambda b, i: (0, 0)),
            pl.BlockSpec((D_MODEL, D_MODEL), lambda b, i: (0, 0)),
            pl.BlockSpec((None, mem_len, 2 * D_MODEL), lambda b, i: (b, 0, 0)),
            pl.BlockSpec((D_MODEL, D_MODEL), lambda b, i: (0, 0)),
        ],
        out_specs=pl.BlockSpec((None, tm, D_MODEL), lambda b, i: (b, i, 0)),
        out_shape=jax.ShapeDtypeStruct((batch, seq, D_MODEL), F32),
        compiler_params=_params(("parallel", "parallel")),
        name="mem_attn",
    )(x3, ret_out, att_out, w_a, w_b, gain.reshape(1, D_MODEL), w_q, kv3, w_o)


def _bitonic_sort_desc(v):
    d = len(v) // 2
    while d >= 1:
        for k in range(len(v)):
            if k & d == 0:
                v[k], v[k + d] = jnp.maximum(v[k], v[k + d]), jnp.minimum(v[k], v[k + d])
        d //= 2


def _top16_pair_sums(t1, t2):
    top = [t1[0] + t2[q] for q in range(PEER_TOPK)]
    lists = [[t1[p] + t2[q] for q in range(PEER_TOPK // (p + 1))] for p in range(1, PEER_TOPK // 2)]
    lists.append([t1[p] + t2[0] for p in range(PEER_TOPK // 2, PEER_TOPK)])
    for other in lists:
        for k in range(PEER_TOPK - len(other), PEER_TOPK):
            top[k] = jnp.maximum(top[k], other[PEER_TOPK - 1 - k])
        _bitonic_sort_desc(top)
    return top


def _oddeven_merge_sort_pairs(n):
    pairs = []
    p = 1
    while p < n:
        k = p
        while k >= 1:
            for j in range(k % p, n - k, 2 * k):
                for i in range(min(k, n - j - k)):
                    if (i + j) // (p * 2) == (i + j + k) // (p * 2):
                        pairs.append((i + j, i + j + k))
            k //= 2
        p *= 2
    return pairs


def _sorted_top16(s_ref, hd, lanes):
    v = [s_ref[hd, 8 * i:8 * (i + 1), lanes] for i in range(PEER_TOPK)]

    def exchange(i, j):
        v[i], v[j] = jnp.maximum(v[i], v[j]), jnp.minimum(v[i], v[j])

    for i, j in _oddeven_merge_sort_pairs(PEER_TOPK):
        exchange(i, j)
    for shift in (4, 2, 1):
        v = [jnp.maximum(v[k], pltpu.roll(v[PEER_TOPK - 1 - k], shift, 0)) for k in range(PEER_TOPK)]
        _bitonic_sort_desc(v)
    return v


def _gelu_tanh(x):
    k0 = -2.0 * math.sqrt(2.0 / math.pi) * math.log2(math.e)
    return x / (1.0 + jnp.exp2(x * (k0 + (k0 * 0.044715) * (x * x))))


def _twin_bf16_words(v):
    bits = lax.bitcast_convert_type(v.astype(BF16).astype(F32), jnp.uint32)
    return bits | (bits >> 16)


def _peer_kernel(x_ref, g_ref, og_ref, wqt_ref, kbd_ref, u_ref, vt_ref, o_ref,
                 ht_ref, qt_ref, s1_ref, s2_ref, r2_ref, b_ref, c_ref, a_ref, t1_ref, t2_ref,
                 cc_ref, invz_ref, acc_ref, *, tm, ec, sub, norm_output):
    e = pl.program_id(1)
    n_chunks = pl.num_programs(1)
    slabs = ec // PEER_N_KEYS
    n_lane_blocks = tm // LANES

    zero = jnp.zeros((), BF16)

    def row_tile(words):
        return pltpu.bitcast(jnp.broadcast_to(words, (PEER_N_KEYS // 2, tm)), BF16)

    def gate_slab(il):
        grp = e * (slabs // 8) + il // 8
        wt = None
        for hd in range(PEER_HEADS):
            crow = row_tile(c_ref[hd, grp, il % 8:il % 8 + 1, :])
            arow = row_tile(a_ref[hd, grp, il % 8:il % 8 + 1, :])
            term = arow * jnp.where(r2_ref[hd] < crow, b_ref[hd], zero)
            wt = term if wt is None else wt + term
        return wt

    @pl.when(e == 0)
    def _select():
        ht_ref[...] = _rms(x_ref[...], g_ref[...]).T.astype(BF16)

        qt_ref[...] = jnp.dot(wqt_ref[...], ht_ref[...], preferred_element_type=F32).astype(BF16)

        def scores(hd, carry):
            rows = pl.ds(pl.multiple_of(hd * PEER_QUERY_DIM, PEER_QUERY_DIM), PEER_QUERY_DIM)
            s12 = jnp.dot(kbd_ref[...], qt_ref[rows, :], preferred_element_type=F32)
            s1_ref[hd] = s12[:PEER_N_KEYS]
            s2_ref[hd] = s12[PEER_N_KEYS:]
            return carry

        lax.fori_loop(0, PEER_HEADS, scores, 0, unroll=True)

        for hd in range(PEER_HEADS):
            def extract(lb, carry, hd=hd):
                lanes = pl.ds(pl.multiple_of(lb * LANES, LANES), LANES)
                top1 = _sorted_top16(s1_ref, hd, lanes)
                for it in range(PEER_TOPK):
                    t1_ref[it, hd:hd + 1, lanes] = top1[it][0:1]
                top2 = _sorted_top16(s2_ref, hd, lanes)
                for it in range(PEER_TOPK):
                    t2_ref[it, hd:hd + 1, lanes] = top2[it][0:1]
                top2 = [jnp.concatenate([t, t], axis=0) for t in top2]
                for grp in range(PEER_N_KEYS // 16):
                    rows = slice(grp * 16, (grp + 1) * 16)
                    s2 = s2_ref[hd, rows, lanes]
                    m8 = s2 >= top2[7]
                    m4 = s2 >= jnp.where(m8, top2[3], top2[11])
                    m2 = s2 >= jnp.where(m8, jnp.where(m4, top2[1], top2[5]),
                                         jnp.where(m4, top2[9], top2[13]))
                    m1 = s2 >= jnp.where(
                        m8,
                        jnp.where(m4, jnp.where(m2, top2[0], top2[2]), jnp.where(m2, top2[4], top2[6])),
                        jnp.where(m4, jnp.where(m2, top2[8], top2[10]), jnp.where(m2, top2[12], top2[14])))
                    rank = (jnp.where(m8, 0.0, 8.0) + jnp.where(m4, 0.0, 4.0)
                            + jnp.where(m2, 0.0, 2.0) + jnp.where(m1, 0.0, 1.0))
                    rank = jnp.where(s2 >= top2[PEER_TOPK - 1], rank, float(PEER_N_KEYS))
                    r2_ref[hd, rows, lanes] = rank.astype(BF16)
                    b_ref[hd, rows, lanes] = jnp.exp(s2 - top2[0]).astype(BF16)
                return carry

            lax.fori_loop(0, n_lane_blocks, extract, 0)

        def compact(lb, carry):
            lanes = pl.ds(pl.multiple_of(lb * LANES, LANES), LANES)
            t1 = [t1_ref[p, :, lanes] for p in range(PEER_TOPK)]
            t2 = [t2_ref[q, :, lanes] for q in range(PEER_TOPK)]
            best = _top16_pair_sums(t1, t2)
            tau = best[PEER_TOPK - 1]
            z = jnp.ones_like(tau)
            for it in range(1, PEER_TOPK):
                z = z + jnp.exp(best[it] - best[0])
            invz_ref[:, lanes] = 1.0 / z
            for p in range(PEER_TOPK):
                cc = jnp.zeros_like(tau)
                for q in range(PEER_TOPK // (p + 1)):
                    cc = cc + jnp.where(t1[p] + t2[q] >= tau, 1.0, 0.0)
                cc_ref[p, :, lanes] = cc
            return carry

        lax.fori_loop(0, n_lane_blocks, compact, 0)

        for hd in range(PEER_HEADS):
            def rows(lb, carry, hd=hd):
                lanes = pl.ds(pl.multiple_of(lb * LANES, LANES), LANES)
                s1 = s1_ref[hd, :, lanes]
                cnt = jnp.zeros_like(s1)
                for p in reversed(range(PEER_TOPK)):
                    cnt = jnp.where(s1 >= t1_ref[p, hd:hd + 1, lanes], cc_ref[p, hd:hd + 1, lanes], cnt)
                a = jnp.exp(s1 - t1_ref[0, hd:hd + 1, lanes]) * invz_ref[hd:hd + 1, lanes]
                c_ref[hd, :, :, lanes] = _twin_bf16_words(cnt).reshape(PEER_N_KEYS // 8, 8, LANES)
                a_ref[hd, :, :, lanes] = _twin_bf16_words(a).reshape(PEER_N_KEYS // 8, 8, LANES)
                return carry

            lax.fori_loop(0, n_lane_blocks, rows, 0, unroll=True)
        acc_ref[...] = jnp.zeros_like(acc_ref)

    sub_slabs = sub // PEER_N_KEYS
    pieces = []
    for s in range(ec // sub):
        act = jnp.dot(u_ref[s * sub:(s + 1) * sub, :], ht_ref[...], preferred_element_type=F32)
        g = _gelu_tanh(act.astype(BF16))
        for k in range(sub_slabs):
            pieces.append(gate_slab(s * sub_slabs + k) * g[k * PEER_N_KEYS:(k + 1) * PEER_N_KEYS, :])
    p = jnp.concatenate(pieces, axis=0)
    acc_ref[...] += jnp.dot(vt_ref[...], p, preferred_element_type=F32)

    @pl.when(e == n_chunks - 1)
    def _finish():
        y = x_ref[...] + acc_ref[...].T
        o_ref[...] = _rms(y, og_ref[...]) if norm_output else y


def _peer(x2, gain, out_gain, wq_t, keys_bd, u, v_t, norm_output, tm=512, ec=2048, sub=128):
    n = x2.shape[0]
    n_chunks = u.shape[0] // ec
    assert ec % (8 * PEER_N_KEYS) == 0 and ec % sub == 0 and sub % PEER_N_KEYS == 0
    kernel = functools.partial(_peer_kernel, tm=tm, ec=ec, sub=sub, norm_output=norm_output)
    head_rows = pltpu.VMEM((PEER_HEADS, PEER_N_KEYS, tm), F32)
    rank_rows = pltpu.VMEM((PEER_TOPK, PEER_HEADS, tm), F32)
    return pl.pallas_call(
        kernel,
        grid=(n // tm, n_chunks),
        in_specs=[
            pl.BlockSpec((tm, D_MODEL), lambda i, e: (i, 0)),
            pl.BlockSpec((1, D_MODEL), lambda i, e: (0, 0)),
            pl.BlockSpec((1, D_MODEL), lambda i, e: (0, 0)),
            pl.BlockSpec((PEER_HEADS * PEER_QUERY_DIM, D_MODEL), lambda i, e: (0, 0)),
            pl.BlockSpec((2 * PEER_N_KEYS, PEER_QUERY_DIM), lambda i, e: (0, 0)),
            pl.BlockSpec((ec, D_MODEL), lambda i, e: (e, 0)),
            pl.BlockSpec((D_MODEL, ec), lambda i, e: (0, e)),
        ],
        out_specs=pl.BlockSpec((tm, D_MODEL), lambda i, e: (i, 0)),
        out_shape=jax.ShapeDtypeStruct((n, D_MODEL), F32),
        scratch_shapes=[
            pltpu.VMEM((D_MODEL, tm), BF16),
            pltpu.VMEM((PEER_HEADS * PEER_QUERY_DIM, tm), BF16),
            head_rows,
            head_rows,
            pltpu.VMEM((PEER_HEADS, PEER_N_KEYS, tm), BF16),
            pltpu.VMEM((PEER_HEADS, PEER_N_KEYS, tm), BF16),
            pltpu.VMEM((PEER_HEADS, PEER_N_KEYS // 8, 8, tm), jnp.uint32),
            pltpu.VMEM((PEER_HEADS, PEER_N_KEYS // 8, 8, tm), jnp.uint32),
            rank_rows,
            rank_rows,
            rank_rows,
            pltpu.VMEM((PEER_HEADS, tm), F32),
            pltpu.VMEM((D_MODEL, tm), F32),
        ],
        compiler_params=_params(("parallel", "arbitrary")),
        name="peer",
    )(x2, gain.reshape(1, D_MODEL), out_gain.reshape(1, D_MODEL), wq_t, keys_bd, u, v_t)


def kernel(x, mem, norm_mix, w_in, ret_decay_logit, ret_norm_gain, att_norm_gain, w_out,
           norm_mem, norm_mem_kv, w_mem_q, w_mem_kv, w_mem_o, norm_ffn,
           peer_w_query, peer_sub_keys, peer_expert_down, peer_expert_up, norm_final):
    batch, seq, _ = x.shape
    mem_len = mem.shape[1]
    depth = w_in.shape[0]
    assert depth >= 1
    n = batch * seq
    x2 = x.reshape(n, D_MODEL)
    mem2 = mem.reshape(batch * mem_len, D_MODEL)
    for layer in range(depth):
        w_in_l = w_in[layer].astype(BF16)
        proj_ret, proj_att = _in_proj(x2, norm_mix[layer], w_in_l[:, :RET_PROJ], w_in_l[:, RET_PROJ:])
        ret_out = _retention(proj_ret.reshape(batch, seq, RET_PROJ), ret_decay_logit[layer],
                             ret_norm_gain[layer], batch, seq)
        att_out = _dilated_attention(proj_att.reshape(batch, seq, ATT_PROJ), att_norm_gain[layer],
                                     batch, seq)
        w_out_l = w_out[layer].astype(BF16)
        kv = _mem_kv(mem2, norm_mem_kv[layer], w_mem_kv[layer].astype(BF16))
        x3 = _mem_attn(x2.reshape(batch, seq, D_MODEL), ret_out, att_out,
                       w_out_l[:RET_WIDTH], w_out_l[RET_WIDTH:],
                       norm_mem[layer], w_mem_q[layer].astype(BF16),
                       kv.reshape(batch, mem_len, 2 * D_MODEL), w_mem_o[layer].astype(BF16))
        x2 = x3.reshape(n, D_MODEL)
        keys = peer_sub_keys[layer].astype(BF16)
        zeros = jnp.zeros_like(keys[0])
        keys_bd = jnp.concatenate([jnp.concatenate([keys[0], zeros], axis=1),
                                   jnp.concatenate([zeros, keys[1]], axis=1)], axis=0)
        x2 = _peer(x2, norm_ffn[layer], norm_final,
                   peer_w_query[layer].T.astype(BF16), keys_bd,
                   peer_expert_down[layer].astype(BF16),
                   peer_expert_up[layer].T.astype(BF16),
                   norm_output=(layer == depth - 1))
    return x2.reshape(batch, seq, D_MODEL)
```
